```python
import math
import jax, jax.numpy as jnp
from jax import lax
import numpy as np

D_MODEL = 1024
BATCH = 4
SEQ = 8192
DEPTH = 1

HEAD_DIM = 64
A_HEADS = 8
A_KV_HEADS = 2
A_GROUP = A_HEADS // A_KV_HEADS
B_HEADS = 8
A_Q_W = A_HEADS * HEAD_DIM
A_KV_W = A_KV_HEADS * HEAD_DIM
B_W = B_HEADS * HEAD_DIM
MIX_W = A_Q_W + B_W
D_IN = A_Q_W + 2 * A_KV_W + 3 * B_W
WINDOW = 128
BLK = 128
ROPE_THETA = 500000.0
ROT_DIM = HEAD_DIM // 4
GRID_W = 64
NA_KH = 8
NA_KW = 16
D_FF = 2816
CONV_W = 3
EPS = 1e-6

kernel_name = "hybrid_window_gqa_neighbourhood_convffn"


def rmsnorm(x, g):
    xf = x.astype(jnp.float32)
    y = xf * lax.rsqrt(jnp.mean(xf * xf, axis=-1, keepdims=True) + EPS)
    return (y * g.astype(jnp.float32)).astype(x.dtype)


def partial_rope(t, cos, sin):
    half = ROT_DIM // 2
    x1 = t[..., :half]
    x2 = t[..., half:ROT_DIM]
    return jnp.concatenate([x1 * cos - x2 * sin, x2 * cos + x1 * sin, t[..., ROT_DIM:]], axis=-1)


def window_gqa_sink(q, k, v, sink):
    B, S = q.shape[0], q.shape[1]
    nb = S // BLK
    span = BLK + 2 * WINDOW
    scale = 1.0 / math.sqrt(HEAD_DIM)
    qb = q.reshape(B, nb, BLK, A_KV_HEADS, A_GROUP, HEAD_DIM)
    qb = jnp.moveaxis(qb, 1, 0)
    pad = ((0, 0), (WINDOW, WINDOW), (0, 0), (0, 0))
    kp = jnp.pad(k, pad)
    vp = jnp.pad(v, pad)
    sink_gr = sink.astype(jnp.float32).reshape(A_KV_HEADS, A_GROUP)[None, :, :, None]

    def block(args):
        q_i, i = args
        start = i * BLK
        k_i = lax.dynamic_slice_in_dim(kp, start, span, axis=1)
        v_i = lax.dynamic_slice_in_dim(vp, start, span, axis=1)
        s = jnp.einsum('bqgrd,bkgd->bgrqk', q_i, k_i).astype(jnp.float32) * scale
        qpos = i * BLK + jnp.arange(BLK)
        kpos = i * BLK - WINDOW + jnp.arange(span)
        valid = (jnp.abs(qpos[:, None] - kpos[None, :]) <= WINDOW) & (kpos[None, :] >= 0) & (kpos[None, :] < S)
        s = jnp.where(valid[None, None, None], s, -1e30)
        m = jnp.maximum(jnp.max(s, axis=-1), sink_gr)
        p = jnp.exp(s - m[..., None])
        denom = jnp.sum(p, axis=-1) + jnp.exp(sink_gr - m)
        o = jnp.einsum('bgrqk,bkgd->bgrqd', p.astype(v_i.dtype), v_i)
        o = o / denom[..., None].astype(o.dtype)
        return jnp.transpose(o, (0, 3, 1, 2, 4))

    out = lax.map(block, (qb, jnp.arange(nb, dtype=jnp.int32)))
    out = jnp.moveaxis(out, 0, 1)
    return out.reshape(B, S, A_Q_W)


def neighbourhood_attention(q, k, v, rpb):
    B, S = q.shape[0], q.shape[1]
    rows = S // GRID_W
    kh = min(NA_KH, rows)
    scale = 1.0 / math.sqrt(HEAD_DIM)
    cs = np.clip(np.arange(GRID_W) - NA_KW // 2, 0, GRID_W - NA_KW)
    col_idx = cs[:, None] + np.arange(NA_KW)[None, :]
    col_off = col_idx - np.arange(GRID_W)[:, None] + (NA_KW - 1)
    qg = jnp.moveaxis(q.reshape(B, rows, GRID_W, B_HEADS, HEAD_DIM), 1, 0)
    kg = k.reshape(B, rows, GRID_W, B_HEADS, HEAD_DIM)
    vg = v.reshape(B, rows, GRID_W, B_HEADS, HEAD_DIM)
    rpb32 = rpb.astype(jnp.float32)

    def row(args):
        q_r, r = args
        rs = jnp.clip(r - kh // 2, 0, rows - kh)
        k_rows = lax.dynamic_slice_in_dim(kg, rs, kh, axis=1)
        v_rows = lax.dynamic_slice_in_dim(vg, rs, kh, axis=1)
        k_win = k_rows[:, :, col_idx]
        v_win = v_rows[:, :, col_idx]
        row_off = rs + jnp.arange(kh) - r + (NA_KH - 1)
        bias = rpb32[:, row_off][:, :, col_off]
        bias = jnp.transpose(bias, (0, 2, 1, 3))
        s = jnp.einsum('bchd,bicjhd->bhcij', q_r, k_win).astype(jnp.float32) * scale + bias[None]
        p = jax.nn.softmax(s.reshape(B, B_HEADS, GRID_W, kh * NA_KW), axis=-1)
        p = p.reshape(B, B_HEADS, GRID_W, kh, NA_KW).astype(v_win.dtype)
        return jnp.einsum('bhcij,bicjhd->bchd', p, v_win)

    out = lax.map(row, (qg, jnp.arange(rows, dtype=jnp.int32)))
    out = jnp.moveaxis(out, 0, 1)
    return out.reshape(B, S, B_W)


def conv_gated_mlp(h, w_gate, w_up, conv_w, conv_b, w_down):
    g = jnp.einsum('bsd,df->bsf', h, w_gate)
    u = jnp.einsum('bsd,df->bsf', h, w_up)
    g = lax.conv_general_dilated(
        g, conv_w[:, None, :], window_strides=(1,),
        padding=((CONV_W // 2, CONV_W // 2),),
        dimension_numbers=('NWC', 'WIO', 'NWC'),
        feature_group_count=D_FF) + conv_b
    return jnp.einsum('bsf,fd->bsd', jax.nn.silu(g) * u, w_down)


def setup_inputs(seed: int = 0) -> dict:
    key = jax.random.key(seed)
    ks = jax.random.split(key, 16)
    f32 = jnp.float32
    nrm = lambda k, shape, s: jax.random.normal(k, shape, f32) * s
    return {
        "x": jax.random.normal(ks[0], (BATCH, SEQ, D_MODEL), f32),
        "norm_mix": 1.0 + nrm(ks[1], (DEPTH, D_MODEL), 0.02),
        "w_in": nrm(ks[2], (DEPTH, D_MODEL, D_IN), D_MODEL ** -0.5),
        "sink": nrm(ks[3], (DEPTH, A_HEADS), 0.5),
        "rpb": nrm(ks[4], (DEPTH, B_HEADS, 2 * NA_KH - 1, 2 * NA_KW - 1), 0.1),
        "norm_a": 1.0 + nrm(ks[5], (DEPTH, A_Q_W), 0.02),
        "norm_b": 1.0 + nrm(ks[6], (DEPTH, B_W), 0.02),
        "w_out": nrm(ks[7], (DEPTH, MIX_W, D_MODEL), MIX_W ** -0.5),
        "norm_ffn": 1.0 + nrm(ks[8], (DEPTH, D_MODEL), 0.02),
        "w_gate": nrm(ks[9], (DEPTH, D_MODEL, D_FF), D_MODEL ** -0.5),
        "w_up": nrm(ks[10], (DEPTH, D_MODEL, D_FF), D_MODEL ** -0.5),
        "conv_w": nrm(ks[11], (DEPTH, CONV_W, D_FF), CONV_W ** -0.5),
        "conv_b": nrm(ks[12], (DEPTH, D_FF), 0.02),
        "w_down": nrm(ks[13], (DEPTH, D_FF, D_MODEL), D_FF ** -0.5),
        "norm_final": 1.0 + nrm(ks[14], (D_MODEL,), 0.02),
    }


def reference(x, norm_mix, w_in, sink, rpb, norm_a, norm_b, w_out, norm_ffn,
              w_gate, w_up, conv_w, conv_b, w_down, norm_final):
    B, S = x.shape[0], x.shape[1]
    pos = jnp.arange(S, dtype=jnp.float32)
    inv_freq = ROPE_THETA ** (-jnp.arange(0, ROT_DIM, 2, dtype=jnp.float32) / ROT_DIM)
    ang = pos[:, None] * inv_freq[None, :]
    cos = jnp.cos(ang)[:, None, :].astype(x.dtype)
    sin = jnp.sin(ang)[:, None, :].astype(x.dtype)
    splits = np.cumsum([A_Q_W, A_KV_W, A_KV_W, B_W, B_W]).tolist()
    h = x
    for l in range(DEPTH):
        hn = rmsnorm(h, norm_mix[l])
        proj = jnp.einsum('bsd,de->bse', hn, w_in[l])
        qa, ka, va, qb, kb, vb = jnp.split(proj, splits, axis=-1)
        qa = partial_rope(qa.reshape(B, S, A_HEADS, HEAD_DIM), cos, sin)
        ka = partial_rope(ka.reshape(B, S, A_KV_HEADS, HEAD_DIM), cos, sin)
        va = va.reshape(B, S, A_KV_HEADS, HEAD_DIM)
        o_a = window_gqa_sink(qa, ka, va, sink[l])
        o_b = neighbourhood_attention(
            qb.reshape(B, S, B_HEADS, HEAD_DIM),
            kb.reshape(B, S, B_HEADS, HEAD_DIM),
            vb.reshape(B, S, B_HEADS, HEAD_DIM), rpb[l])
        mix = jnp.concatenate([rmsnorm(o_a, norm_a[l]), rmsnorm(o_b, norm_b[l])], axis=-1)
        h = h + jnp.einsum('bse,ed->bsd', mix, w_out[l])
        hn = rmsnorm(h, norm_ffn[l])
        h = h + conv_gated_mlp(hn, w_gate[l], w_up[l], conv_w[l], conv_b[l], w_down[l])
    return rmsnorm(h, norm_final)
```

```python
import functools
import math

import numpy as np
import jax
import jax.numpy as jnp
from jax import lax
from jax.experimental import pallas as pl
from jax.experimental.pallas import tpu as pltpu

D_MODEL = 1024
HEAD_DIM = 64
A_HEADS = 8
A_KV_HEADS = 2
B_HEADS = 8
A_Q_W = A_HEADS * HEAD_DIM
A_KV_W = A_KV_HEADS * HEAD_DIM
B_W = B_HEADS * HEAD_DIM
WINDOW = 128
BLK = 128
ROPE_THETA = 500000.0
ROT_DIM = HEAD_DIM // 4
GRID_W = 64
NA_KH = 8
NA_KW = 16
D_FF = 2816
EPS = 1e-6
NEG = -1e30

LANES = 128
VMEM_LIMIT = 56 * 1024 * 1024

F32 = jnp.float32
BF16 = jnp.bfloat16

PROJ_TM = 512
NB_ROWS = 4
NB_TOK = NB_ROWS * GRID_W
NB_WIN = 3 * NB_TOK
FFN_HALO = 16
FFN_CHUNKS = ((0, 512), (512, 1024), (1024, 1536), (1536, 2048), (2048, 2560), (2560, 2816))


def _rms(x, g):
    return x * lax.rsqrt(jnp.mean(x * x, axis=-1, keepdims=True) + EPS) * g


def _proj_kernel(x_ref, g_ref, w_ref, c_ref, s1_ref, s2_ref,
                 qa_ref, kva_ref, qbl_ref, qbr_ref, kb_ref, vb_ref):
    hn = _rms(x_ref[...], g_ref[...]).astype(BF16)
    c = c_ref[...]
    s1 = s1_ref[...]
    s2 = s2_ref[...]
    tm = hn.shape[0]
    left = lax.broadcasted_iota(jnp.int32, (tm, LANES), 1) < HEAD_DIM
    scale = 1.0 / math.sqrt(HEAD_DIM)

    def rope(t):
        return t * c + pltpu.roll(t, 8, 1) * s1 + pltpu.roll(t, LANES - 8, 1) * s2

    def mm(lo, hi):
        return jnp.dot(hn, w_ref[:, lo:hi], preferred_element_type=F32)

    qa = mm(0, A_Q_W)
    for j in range(A_Q_W // LANES):
        sl = slice(j * LANES, (j + 1) * LANES)
        qa_ref[:, sl] = (rope(qa[:, sl]) * scale).astype(BF16)

    kv = mm(A_Q_W, A_Q_W + 2 * A_KV_W)
    ka = rope(kv[:, :LANES])
    va = kv[:, LANES:]
    zero = jnp.zeros_like(ka)
    for base, t in ((0, ka), (4, va)):
        sw = pltpu.roll(t, HEAD_DIM, 1)
        parts = (jnp.where(left, t, zero), jnp.where(left, zero, sw),
                 jnp.where(left, sw, zero), jnp.where(left, zero, t))
        for n, p in enumerate(parts):
            kva_ref[:, (base + n) * LANES:(base + n + 1) * LANES] = p.astype(BF16)

    o = A_Q_W + 2 * A_KV_W
    qb = mm(o, o + B_W) * scale
    for j in range(B_W // LANES):
        sl = slice(j * LANES, (j + 1) * LANES)
        t = qb[:, sl]
        qbl_ref[:, sl] = jnp.where(left, t, zero).astype(BF16)
        qbr_ref[:, sl] = jnp.where(left, zero, t).astype(BF16)
    kb_ref[...] = mm(o + B_W, o + 2 * B_W).astype(BF16)
    vb_ref[...] = mm(o + 2 * B_W, o + 3 * B_W).astype(BF16)


def _proj(x2, g, w, c, s1, s2, seq):
    t = x2.shape[0]
    tm = PROJ_TM
    nseq = seq // tm
    row = lambda i: (i, 0)
    const = lambda i: (0, 0)
    tab = lambda i: (i % nseq, 0)
    outs = [(A_Q_W, row), (8 * LANES, row), (B_W, row), (B_W, row), (B_W, row), (B_W, row)]
    return pl.pallas_call(
        _proj_kernel,
        grid=(t // tm,),
        in_specs=[pl.BlockSpec((tm, D_MODEL), row),
                  pl.BlockSpec((1, D_MODEL), const),
                  pl.BlockSpec(w.shape, const),
                  pl.BlockSpec((tm, LANES), tab),
                  pl.BlockSpec((tm, LANES), tab),
                  pl.BlockSpec((tm, LANES), tab)],
        out_specs=[pl.BlockSpec((tm, wd), im) for wd, im in outs],
        out_shape=[jax.ShapeDtypeStruct((t, wd), BF16) for wd, _ in outs],
        compiler_params=pltpu.CompilerParams(
            dimension_semantics=("arbitrary",), vmem_limit_bytes=VMEM_LIMIT),
        name="proj",
    )(x2, g, w, c, s1, s2)


def _attn_a_kernel(sink_ref, q_ref, kp_ref, kc_ref, kn_ref, g_ref, o_ref):
    i = pl.program_id(1)
    nblk = pl.num_programs(1)
    span = BLK + 2 * WINDOW
    a = lax.broadcasted_iota(jnp.int32, (BLK, span), 0)
    b = lax.broadcasted_iota(jnp.int32, (BLK, span), 1)
    d = b - a
    lo = jnp.where(i == 0, WINDOW, 0)
    hi = jnp.where(i == nblk - 1, WINDOW + BLK, span)
    valid = (d >= 0) & (d <= 2 * WINDOW) & (b >= lo) & (b < hi)
    left = lax.broadcasted_iota(jnp.int32, (BLK, LANES), 1) < HEAD_DIM
    blocks = (kp_ref, kc_ref, kn_ref)

    outs = []
    for g in range(A_KV_HEADS):
        def stack(col):
            return jnp.concatenate(
                [r[:, (col + n) * LANES:(col + n + 1) * LANES] for n in range(2) for r in blocks], axis=0)
        kbd = stack(2 * g)
        vbd = stack(4 + 2 * g)
        q2 = jnp.concatenate([q_ref[:, (2 * g + p) * LANES:(2 * g + p + 1) * LANES] for p in range(2)], axis=0)
        s = lax.dot_general(q2, kbd, (((1,), (1,)), ((), ())), preferred_element_type=F32)
        prow = []
        dens = []
        for p in range(2):
            pcol = []
            dpair = []
            for hh in range(2):
                sk = sink_ref[4 * g + 2 * p + hh]
                sq = jnp.where(valid, s[p * BLK:(p + 1) * BLK, hh * span:(hh + 1) * span], NEG)
                m = jnp.maximum(jnp.max(sq, axis=-1, keepdims=True), sk)
                e = jnp.exp(sq - m)
                dpair.append(jnp.sum(e, axis=-1, keepdims=True) + jnp.exp(sk - m))
                pcol.append(e.astype(BF16))
            prow.append(jnp.concatenate(pcol, axis=1))
            dens.append(jnp.where(left, dpair[0], dpair[1]))
        pm = jnp.concatenate(prow, axis=0)
        o2 = jnp.dot(pm, vbd, preferred_element_type=F32)
        for p in range(2):
            outs.append(o2[p * BLK:(p + 1) * BLK] / dens[p])
    o = jnp.concatenate(outs, axis=1)
    o_ref[...] = _rms(o, g_ref[...]).astype(BF16)


def _attn_a(sink, qa, kva, g, batch, seq):
    t = qa.shape[0]
    nblk = seq // BLK
    cur = lambda b, i: (b * nblk + i, 0)
    prev = lambda b, i: (b * nblk + jnp.maximum(i - 1, 0), 0)
    nxt = lambda b, i: (b * nblk + jnp.minimum(i + 1, nblk - 1), 0)
    const = lambda b, i: (0, 0)
    kw = kva.shape[1]
    return pl.pallas_call(
        _attn_a_kernel,
        grid=(batch, nblk),
        in_specs=[pl.BlockSpec(memory_space=pltpu.SMEM),
                  pl.BlockSpec((BLK, A_Q_W), cur),
                  pl.BlockSpec((BLK, kw), prev),
                  pl.BlockSpec((BLK, kw), cur),
                  pl.BlockSpec((BLK, kw), nxt),
                  pl.BlockSpec((1, A_Q_W), const)],
        out_specs=pl.BlockSpec((BLK, A_Q_W), cur),
        out_shape=jax.ShapeDtypeStruct((t, A_Q_W), BF16),
        compiler_params=pltpu.CompilerParams(
            dimension_semantics=("arbitrary", "arbitrary"), vmem_limit_bytes=VMEM_LIMIT),
        name="attn_a",
    )(sink, qa, kva, kva, kva, g)


def _attn_b_kernel(ql_ref, qr_ref, kp_ref, kc_ref, kn_ref, vp_ref, vc_ref, vn_ref,
                   bias_ref, g_ref, o_ref):
    left = lax.broadcasted_iota(jnp.int32, (NB_TOK, LANES), 1) < HEAD_DIM
    outs = []
    for p in range(B_HEADS // 2):
        sl = slice(p * LANES, (p + 1) * LANES)
        q2 = jnp.concatenate([ql_ref[:, sl], qr_ref[:, sl]], axis=0)
        kw = jnp.concatenate([kp_ref[:, sl], kc_ref[:, sl], kn_ref[:, sl]], axis=0)
        vw = jnp.concatenate([vp_ref[:, sl], vc_ref[:, sl], vn_ref[:, sl]], axis=0)
        s = lax.dot_general(q2, kw, (((1,), (1,)), ((), ())), preferred_element_type=F32)
        ps = []
        ls = []
        for hh in range(2):
            sq = s[hh * NB_TOK:(hh + 1) * NB_TOK] + bias_ref[0, 2 * p + hh]
            m = jnp.max(sq, axis=-1, keepdims=True)
            e = jnp.exp(sq - m)
            ls.append(jnp.sum(e, axis=-1, keepdims=True))
            ps.append(e.astype(BF16))
        o2 = jnp.dot(jnp.concatenate(ps, axis=0), vw, preferred_element_type=F32)
        outs.append(jnp.where(left, o2[:NB_TOK] / ls[0], o2[NB_TOK:] / ls[1]))
    o = jnp.concatenate(outs, axis=1)
    o_ref[...] = _rms(o, g_ref[...]).astype(BF16)


def _attn_b(qbl, qbr, kb, vb, bias, g, batch, seq):
    t = qbl.shape[0]
    ngrp = seq // NB_TOK
    cur = lambda b, j: (b * ngrp + j, 0)
    prev = lambda b, j: (b * ngrp + jnp.maximum(j - 1, 0), 0)
    nxt = lambda b, j: (b * ngrp + jnp.minimum(j + 1, ngrp - 1), 0)
    cfg = lambda b, j: (jnp.where(j == 0, 0, jnp.where(j == ngrp - 1, 2, 1)), 0, 0, 0)
    const = lambda b, j: (0, 0)
    blk = lambda im: pl.BlockSpec((NB_TOK, B_W), im)
    return pl.pallas_call(
        _attn_b_kernel,
        grid=(batch, ngrp),
        in_specs=[blk(cur), blk(cur), blk(prev), blk(cur), blk(nxt), blk(prev), blk(cur), blk(nxt),
                  pl.BlockSpec((1, B_HEADS, NB_TOK, NB_WIN), cfg),
                  pl.BlockSpec((1, B_W), const)],
        out_specs=blk(cur),
        out_shape=jax.ShapeDtypeStruct((t, B_W), BF16),
        compiler_params=pltpu.CompilerParams(
            dimension_semantics=("arbitrary", "arbitrary"), vmem_limit_bytes=VMEM_LIMIT),
        name="attn_b",
    )(qbl, qbr, kb, kb, kb, vb, vb, vb, bias, g)


def _nb_bias(rpb_l):
    rows = NB_ROWS
    qr = np.arange(rows)[:, None, None, None]
    cq = np.arange(GRID_W)[None, :, None, None]
    wr = np.arange(3 * rows)[None, None, :, None]
    kc = np.arange(GRID_W)[None, None, None, :]
    cs = np.clip(cq - NA_KW // 2, 0, GRID_W - NA_KW)
    col_ok = (kc >= cs) & (kc < cs + NA_KW)
    row_off = np.clip(wr - qr + (NA_KH - 1) - rows, 0, 2 * NA_KH - 2)
    col_off = np.clip(kc - cq + (NA_KW - 1), 0, 2 * NA_KW - 2)
    row_ok = np.stack([
        np.broadcast_to(wr >= rows, (rows, 1, 3 * rows, 1)),
        np.broadcast_to((wr - qr >= 0) & (wr - qr < NA_KH), (rows, 1, 3 * rows, 1)),
        np.broadcast_to(wr < NA_KH, (rows, 1, 3 * rows, 1)),
    ])
    ok = row_ok[:, None] & col_ok[None, None]
    ro = np.broadcast_to(row_off, (rows, GRID_W, 3 * rows, GRID_W))
    co = np.broadcast_to(col_off, (rows, GRID_W, 3 * rows, GRID_W))
    vals = rpb_l.astype(F32)[:, ro, co]
    full = jnp.where(jnp.asarray(ok), vals[None], NEG)
    return full.reshape(3, B_HEADS, NB_TOK, NB_WIN)


def _oproj_kernel(a_ref, b_ref, x_ref, w_ref, h_ref):
    acc = jnp.dot(a_ref[...], w_ref[:A_Q_W, :], preferred_element_type=F32)
    acc = acc + jnp.dot(b_ref[...], w_ref[A_Q_W:, :], preferred_element_type=F32)
    h_ref[...] = x_ref[...] + acc


def _oproj(ma, mb, x2, w):
    t = x2.shape[0]
    tm = PROJ_TM
    row = lambda i: (i, 0)
    const = lambda i: (0, 0)
    return pl.pallas_call(
        _oproj_kernel,
        grid=(t // tm,),
        in_specs=[pl.BlockSpec((tm, A_Q_W), row), pl.BlockSpec((tm, B_W), row),
                  pl.BlockSpec((tm, D_MODEL), row), pl.BlockSpec(w.shape, const)],
        out_specs=pl.BlockSpec((tm, D_MODEL), row),
        out_shape=jax.ShapeDtypeStruct((t, D_MODEL), F32),
        compiler_params=pltpu.CompilerParams(
            dimension_semantics=("arbitrary",), vmem_limit_bytes=VMEM_LIMIT),
        name="oproj",
    )(ma, mb, x2, w)


def _ffn_kernel(nseq, final, hp_ref, h_ref, hn_ref, gf_ref, wg_ref, wu_ref, cw_ref, cb_ref, wd_ref, gl_ref,
                o_ref, xn_ref, gs_ref, acc_ref):
    i = pl.program_id(0)
    tm = h_ref.shape[0]
    hal = FFN_HALO
    gf = gf_ref[...]
    xn_ref[0:hal, :] = _rms(hp_ref[...], gf).astype(BF16)
    xn_ref[hal:hal + tm, :] = _rms(h_ref[...], gf).astype(BF16)
    xn_ref[hal + tm:, :] = _rms(hn_ref[...], gf).astype(BF16)
    first = (i % nseq) == 0
    last = (i % nseq) == nseq - 1

    for n, (lo, hi) in enumerate(FFN_CHUNKS):
        w = hi - lo
        gs_ref[:, :w] = jnp.dot(xn_ref[...], wg_ref[:, lo:hi], preferred_element_type=F32)

        @pl.when(first)
        def _():
            gs_ref[hal - 8:hal, :] = jnp.zeros((8, gs_ref.shape[1]), F32)

        @pl.when(last)
        def _():
            gs_ref[hal + tm:hal + tm + 8, :] = jnp.zeros((8, gs_ref.shape[1]), F32)

        gc = (gs_ref[hal - 1:hal - 1 + tm, :w] * cw_ref[0:1, lo:hi]
              + gs_ref[hal:hal + tm, :w] * cw_ref[1:2, lo:hi]
              + gs_ref[hal + 1:hal + 1 + tm, :w] * cw_ref[2:3, lo:hi]
              + cb_ref[:, lo:hi])
        u = jnp.dot(xn_ref[hal:hal + tm, :], wu_ref[:, lo:hi], preferred_element_type=F32)
        act = (gc / (1.0 + jnp.exp(-gc)) * u).astype(BF16)
        part = jnp.dot(act, wd_ref[lo:hi, :], preferred_element_type=F32)
        if n == 0:
            acc_ref[...] = part
        else:
            acc_ref[...] += part

    out = h_ref[...] + acc_ref[...]
    o_ref[...] = _rms(out, gl_ref[...]) if final else out


def _ffn(h, gf, wg, wu, cw, cb, wd, gl, seq, final):
    t = h.shape[0]
    tm = PROJ_TM
    nseq = seq // tm
    r = tm // FFN_HALO
    nhal = t // FFN_HALO
    row = lambda i: (i, 0)
    prev = lambda i: (jnp.maximum(i * r - 1, 0), 0)
    nxt = lambda i: (jnp.minimum((i + 1) * r, nhal - 1), 0)
    const = lambda i: (0, 0)
    once = lambda shape: pl.BlockSpec(shape, const, pipeline_mode=pl.Buffered(1))
    return pl.pallas_call(
        functools.partial(_ffn_kernel, nseq, final),
        grid=(t // tm,),
        in_specs=[pl.BlockSpec((FFN_HALO, D_MODEL), prev),
                  pl.BlockSpec((tm, D_MODEL), row),
                  pl.BlockSpec((FFN_HALO, D_MODEL), nxt),
                  once((1, D_MODEL)), once(wg.shape), once(wu.shape), once(cw.shape),
                  once(cb.shape), once(wd.shape), once((1, D_MODEL))],
        out_specs=pl.BlockSpec((tm, D_MODEL), row),
        out_shape=jax.ShapeDtypeStruct((t, D_MODEL), F32),
        scratch_shapes=[pltpu.VMEM((tm + 2 * FFN_HALO, D_MODEL), BF16),
                        pltpu.VMEM((tm + 2 * FFN_HALO, 512), F32),
                        pltpu.VMEM((tm, D_MODEL), F32)],
        compiler_params=pltpu.CompilerParams(
            dimension_semantics=("arbitrary",), vmem_limit_bytes=VMEM_LIMIT),
        name="ffn",
    )(h, h, h, gf, wg, wu, cw, cb, wd, gl)


def _rope_tables(seq):
    pos = jnp.arange(seq, dtype=F32)
    inv_freq = ROPE_THETA ** (-jnp.arange(0, ROT_DIM, 2, dtype=F32) / ROT_DIM)
    ang = pos[:, None] * inv_freq[None, :]
    cos = jnp.cos(ang)
    sin = jnp.sin(ang)
    half = ROT_DIM // 2
    ones = jnp.ones((seq, HEAD_DIM - ROT_DIM), F32)
    zeros = jnp.zeros((seq, HEAD_DIM - ROT_DIM), F32)
    zh = jnp.zeros((seq, half), F32)
    c = jnp.concatenate([cos, cos, ones], axis=1)
    s1 = jnp.concatenate([zh, sin, zeros], axis=1)
    s2 = jnp.concatenate([-sin, zh, zeros], axis=1)
    rep = LANES // HEAD_DIM
    return tuple(jnp.tile(t, (1, rep)) for t in (c, s1, s2))


def kernel(x, norm_mix, w_in, sink, rpb, norm_a, norm_b, w_out, norm_ffn,
           w_gate, w_up, conv_w, conv_b, w_down, norm_final):
    batch, seq, d = x.shape
    depth = w_in.shape[0]
    assert d == D_MODEL and seq % PROJ_TM == 0 and seq % NB_TOK == 0 and seq // GRID_W >= 3 * NB_ROWS
    c, s1, s2 = _rope_tables(seq)
    h = x.reshape(batch * seq, d)
    for l in range(depth):
        qa, kva, qbl, qbr, kb, vb = _proj(h, norm_mix[l][None], w_in[l].astype(BF16), c, s1, s2, seq)
        ma = _attn_a(sink[l].astype(F32), qa, kva, norm_a[l][None].astype(F32), batch, seq)
        mb = _attn_b(qbl, qbr, kb, vb, _nb_bias(rpb[l]), norm_b[l][None].astype(F32), batch, seq)
        h = _oproj(ma, mb, h, w_out[l].astype(BF16))
        h = _ffn(h, norm_ffn[l][None], w_gate[l].astype(BF16), w_up[l].astype(BF16),
                 conv_w[l], conv_b[l][None], w_down[l].astype(BF16), norm_final[None], seq,
                 final=(l == depth - 1))
    return h.reshape(batch, seq, d)
```

```python
import functools
import math

import numpy as np
import jax
import jax.numpy as jnp
from jax import lax
from jax.experimental import pallas as pl
from jax.experimental.pallas import tpu as pltpu

D_MODEL = 1024
HEAD_DIM = 64
A_HEADS = 8
A_KV_HEADS = 2
B_HEADS = 8
A_Q_W = A_HEADS * HEAD_DIM
A_KV_W = A_KV_HEADS * HEAD_DIM
B_W = B_HEADS * HEAD_DIM
WINDOW = 128
BLK = 128
ROPE_THETA = 500000.0
ROT_DIM = HEAD_DIM // 4
GRID_W = 64
NA_KH = 8
NA_KW = 16
D_FF = 2816
EPS = 1e-6
NEG = -1e30

LANES = 128
VMEM_LIMIT = 56 * 1024 * 1024

F32 = jnp.float32
BF16 = jnp.bfloat16

PROJ_TM = 512
A_SUB = 4
FFN_TM = 512
FFN_TF = 512
NB_ROWS = 4
NB_TOK = NB_ROWS * GRID_W
NB_WIN = 3 * NB_TOK
FFN_HALO = 16
FFN_CHUNKS = tuple((lo, min(lo + FFN_TF, D_FF)) for lo in range(0, D_FF, FFN_TF))


def _rms(x, g):
    return x * lax.rsqrt(jnp.mean(x * x, axis=-1, keepdims=True) + EPS) * g


def _proj_kernel(x_ref, g_ref, w_ref, c_ref, s1_ref, s2_ref,
                 qa_ref, kva_ref, qbl_ref, qbr_ref, kb_ref, vb_ref):
    hn = _rms(x_ref[...], g_ref[...]).astype(BF16)
    c = c_ref[...]
    s1 = s1_ref[...]
    s2 = s2_ref[...]
    tm = hn.shape[0]
    left = lax.broadcasted_iota(jnp.int32, (tm, LANES), 1) < HEAD_DIM
    scale = 1.0 / math.sqrt(HEAD_DIM)

    def rope(t):
        return t * c + pltpu.roll(t, 8, 1) * s1 + pltpu.roll(t, LANES - 8, 1) * s2

    def mm(lo, hi):
        return jnp.dot(hn, w_ref[:, lo:hi], preferred_element_type=F32)

    qa = mm(0, A_Q_W)
    for j in range(A_Q_W // LANES):
        sl = slice(j * LANES, (j + 1) * LANES)
        qa_ref[:, sl] = (rope(qa[:, sl]) * scale).astype(BF16)

    kv = mm(A_Q_W, A_Q_W + 2 * A_KV_W)
    ka = rope(kv[:, :LANES])
    va = kv[:, LANES:]
    zero = jnp.zeros_like(ka)
    for base, t in ((0, ka), (4, va)):
        sw = pltpu.roll(t, HEAD_DIM, 1)
        parts = (jnp.where(left, t, zero), jnp.where(left, zero, sw),
                 jnp.where(left, sw, zero), jnp.where(left, zero, t))
        for n, p in enumerate(parts):
            kva_ref[:, (base + n) * LANES:(base + n + 1) * LANES] = p.astype(BF16)

    o = A_Q_W + 2 * A_KV_W
    qb = mm(o, o + B_W) * scale
    for j in range(B_W // LANES):
        sl = slice(j * LANES, (j + 1) * LANES)
        t = qb[:, sl]
        qbl_ref[:, sl] = jnp.where(left, t, zero).astype(BF16)
        qbr_ref[:, sl] = jnp.where(left, zero, t).astype(BF16)
    kb_ref[...] = mm(o + B_W, o + 2 * B_W).astype(BF16)
    vb_ref[...] = mm(o + 2 * B_W, o + 3 * B_W).astype(BF16)


def _proj(x2, g, w, c, s1, s2, seq):
    t = x2.shape[0]
    tm = PROJ_TM
    nseq = seq // tm
    row = lambda i: (i, 0)
    const = lambda i: (0, 0)
    tab = lambda i: (i % nseq, 0)
    outs = [(A_Q_W, row), (8 * LANES, row), (B_W, row), (B_W, row), (B_W, row), (B_W, row)]
    return pl.pallas_call(
        _proj_kernel,
        grid=(t // tm,),
        in_specs=[pl.BlockSpec((tm, D_MODEL), row),
                  pl.BlockSpec((1, D_MODEL), const),
                  pl.BlockSpec(w.shape, const),
                  pl.BlockSpec((tm, LANES), tab),
                  pl.BlockSpec((tm, LANES), tab),
                  pl.BlockSpec((tm, LANES), tab)],
        out_specs=[pl.BlockSpec((tm, wd), im) for wd, im in outs],
        out_shape=[jax.ShapeDtypeStruct((t, wd), BF16) for wd, _ in outs],
        compiler_params=pltpu.CompilerParams(
            dimension_semantics=("arbitrary",), vmem_limit_bytes=VMEM_LIMIT),
        name="proj",
    )(x2, g, w, c, s1, s2)


def _attn_a_kernel(sink_ref, q_ref, kp_ref, kc_ref, kn_ref, g_ref, o_ref):
    i = pl.program_id(1)
    nstep = pl.num_programs(1)
    span = BLK + 2 * WINDOW
    a = lax.broadcasted_iota(jnp.int32, (BLK, span), 0)
    b = lax.broadcasted_iota(jnp.int32, (BLK, span), 1)
    d = b - a
    band = (d >= 0) & (d <= 2 * WINDOW)
    left = lax.broadcasted_iota(jnp.int32, (BLK, LANES), 1) < HEAD_DIM

    def key_block(idx, cols):
        if idx == 0:
            return kp_ref[:, cols]
        if idx <= A_SUB:
            return kc_ref[(idx - 1) * BLK:idx * BLK, cols]
        return kn_ref[:, cols]

    for j in range(A_SUB):
        valid = band
        if j == 0:
            valid = valid & (b >= jnp.where(i == 0, WINDOW, 0))
        if j == A_SUB - 1:
            valid = valid & (b < jnp.where(i == nstep - 1, WINDOW + BLK, span))
        rows = slice(j * BLK, (j + 1) * BLK)
        outs = []
        for g in range(A_KV_HEADS):
            def stack(col):
                return jnp.concatenate(
                    [key_block(j + w, slice((col + n) * LANES, (col + n + 1) * LANES))
                     for n in range(2) for w in range(3)], axis=0)
            kbd = stack(2 * g)
            vbd = stack(4 + 2 * g)
            q2 = jnp.concatenate(
                [q_ref[rows, (2 * g + p) * LANES:(2 * g + p + 1) * LANES] for p in range(2)], axis=0)
            s = lax.dot_general(q2, kbd, (((1,), (1,)), ((), ())), preferred_element_type=F32)
            prow = []
            dens = []
            for p in range(2):
                pcol = []
                dpair = []
                for hh in range(2):
                    sk = sink_ref[4 * g + 2 * p + hh]
                    sq = jnp.where(valid, s[p * BLK:(p + 1) * BLK, hh * span:(hh + 1) * span], NEG)
                    m = jnp.maximum(jnp.max(sq, axis=-1, keepdims=True), sk)
                    e = jnp.exp(sq - m)
                    dpair.append(jnp.sum(e, axis=-1, keepdims=True) + jnp.exp(sk - m))
                    pcol.append(e.astype(BF16))
                prow.append(jnp.concatenate(pcol, axis=1))
                dens.append(jnp.where(left, dpair[0], dpair[1]))
            pm = jnp.concatenate(prow, axis=0)
            o2 = jnp.dot(pm, vbd, preferred_element_type=F32)
            for p in range(2):
                outs.append(o2[p * BLK:(p + 1) * BLK] / dens[p])
        o = jnp.concatenate(outs, axis=1)
        o_ref[rows, :] = _rms(o, g_ref[...]).astype(BF16)


def _attn_a(sink, qa, kva, g, batch, seq):
    t = qa.shape[0]
    nblk = seq // BLK
    nstep = nblk // A_SUB
    cur = lambda b, i: (b * nstep + i, 0)
    prev = lambda b, i: (b * nblk + jnp.maximum(i * A_SUB - 1, 0), 0)
    nxt = lambda b, i: (b * nblk + jnp.minimum((i + 1) * A_SUB, nblk - 1), 0)
    const = lambda b, i: (0, 0)
    kw = kva.shape[1]
    return pl.pallas_call(
        _attn_a_kernel,
        grid=(batch, nstep),
        in_specs=[pl.BlockSpec(memory_space=pltpu.SMEM),
                  pl.BlockSpec((A_SUB * BLK, A_Q_W), cur),
                  pl.BlockSpec((BLK, kw), prev),
                  pl.BlockSpec((A_SUB * BLK, kw), cur),
                  pl.BlockSpec((BLK, kw), nxt),
                  pl.BlockSpec((1, A_Q_W), const)],
        out_specs=pl.BlockSpec((A_SUB * BLK, A_Q_W), cur),
        out_shape=jax.ShapeDtypeStruct((t, A_Q_W), BF16),
        compiler_params=pltpu.CompilerParams(
            dimension_semantics=("arbitrary", "arbitrary"), vmem_limit_bytes=VMEM_LIMIT),
        name="attn_a",
    )(sink, qa, kva, kva, kva, g)


def _attn_b_kernel(ql_ref, qr_ref, kp_ref, kc_ref, kn_ref, vp_ref, vc_ref, vn_ref,
                   bias_ref, g_ref, o_ref):
    left = lax.broadcasted_iota(jnp.int32, (NB_TOK, LANES), 1) < HEAD_DIM
    outs = []
    for p in range(B_HEADS // 2):
        sl = slice(p * LANES, (p + 1) * LANES)
        q2 = jnp.concatenate([ql_ref[:, sl], qr_ref[:, sl]], axis=0)
        kw = jnp.concatenate([kp_ref[:, sl], kc_ref[:, sl], kn_ref[:, sl]], axis=0)
        vw = jnp.concatenate([vp_ref[:, sl], vc_ref[:, sl], vn_ref[:, sl]], axis=0)
        s = lax.dot_general(q2, kw, (((1,), (1,)), ((), ())), preferred_element_type=F32)
        ps = []
        ls = []
        for hh in range(2):
            sq = s[hh * NB_TOK:(hh + 1) * NB_TOK] + bias_ref[0, 2 * p + hh]
            m = jnp.max(sq, axis=-1, keepdims=True)
            e = jnp.exp(sq - m)
            ls.append(jnp.sum(e, axis=-1, keepdims=True))
            ps.append(e.astype(BF16))
        o2 = jnp.dot(jnp.concatenate(ps, axis=0), vw, preferred_element_type=F32)
        outs.append(jnp.where(left, o2[:NB_TOK] / ls[0], o2[NB_TOK:] / ls[1]))
    o = jnp.concatenate(outs, axis=1)
    o_ref[...] = _rms(o, g_ref[...]).astype(BF16)


def _attn_b(qbl, qbr, kb, vb, bias, g, batch, seq):
    t = qbl.shape[0]
    ngrp = seq // NB_TOK
    cur = lambda b, j: (b * ngrp + j, 0)
    prev = lambda b, j: (b * ngrp + jnp.maximum(j - 1, 0), 0)
    nxt = lambda b, j: (b * ngrp + jnp.minimum(j + 1, ngrp - 1), 0)
    cfg = lambda b, j: (jnp.where(j == 0, 0, jnp.where(j == ngrp - 1, 2, 1)), 0, 0, 0)
    const = lambda b, j: (0, 0)
    blk = lambda im: pl.BlockSpec((NB_TOK, B_W), im)
    return pl.pallas_call(
        _attn_b_kernel,
        grid=(batch, ngrp),
        in_specs=[blk(cur), blk(cur), blk(prev), blk(cur), blk(nxt), blk(prev), blk(cur), blk(nxt),
                  pl.BlockSpec((1, B_HEADS, NB_TOK, NB_WIN), cfg),
                  pl.BlockSpec((1, B_W), const)],
        out_specs=blk(cur),
        out_shape=jax.ShapeDtypeStruct((t, B_W), BF16),
        compiler_params=pltpu.CompilerParams(
            dimension_semantics=("arbitrary", "arbitrary"), vmem_limit_bytes=VMEM_LIMIT),
        name="attn_b",
    )(qbl, qbr, kb, kb, kb, vb, vb, vb, bias, g)


def _nb_bias(rpb_l):
    rows, n = NB_ROWS, GRID_W
    r = rpb_l.astype(F32)
    lead = r.shape[:-1]
    pad = n - NA_KW
    v = jnp.concatenate([jnp.broadcast_to(r[..., :1], lead + (pad,)), r,
                         jnp.broadcast_to(r[..., -1:], lead + (pad,))], axis=-1)
    w = jnp.concatenate([v[..., n - 1:], v[..., :1], v[..., :n - 1]], axis=-1)
    t = jnp.tile(w, (1, 1, n))[..., :n * (2 * n - 1)].reshape(lead + (n, 2 * n - 1))[..., :n]
    cq = np.arange(n)[:, None]
    kc = np.arange(n)[None, :]
    cs = np.clip(cq - NA_KW // 2, 0, n - NA_KW)
    t = jnp.where(jnp.asarray((kc >= cs) & (kc < cs + NA_KW)), t, NEG)
    off = NA_KH - 1 - rows
    blocks = jnp.stack([t[:, off - qr:off - qr + 3 * rows] for qr in range(rows)], axis=1)
    blocks = blocks.transpose(0, 1, 3, 2, 4)
    qr = np.arange(rows)[:, None]
    wr = np.arange(3 * rows)[None, :]
    row_ok = np.stack([np.broadcast_to(wr >= rows, (rows, 3 * rows)),
                       (wr - qr >= 0) & (wr - qr < NA_KH),
                       np.broadcast_to(wr < NA_KH, (rows, 3 * rows))])
    full = jnp.where(jnp.asarray(row_ok)[:, None, :, None, :, None], blocks[None], NEG)
    return full.reshape(3, B_HEADS, NB_TOK, NB_WIN)


def _oproj_kernel(a_ref, b_ref, x_ref, w_ref, h_ref):
    acc = jnp.dot(a_ref[...], w_ref[:A_Q_W, :], preferred_element_type=F32)
    acc = acc + jnp.dot(b_ref[...], w_ref[A_Q_W:, :], preferred_element_type=F32)
    h_ref[...] = x_ref[...] + acc


def _oproj(ma, mb, x2, w):
    t = x2.shape[0]
    tm = PROJ_TM
    row = lambda i: (i, 0)
    const = lambda i: (0, 0)
    return pl.pallas_call(
        _oproj_kernel,
        grid=(t // tm,),
        in_specs=[pl.BlockSpec((tm, A_Q_W), row), pl.BlockSpec((tm, B_W), row),
                  pl.BlockSpec((tm, D_MODEL), row), pl.BlockSpec(w.shape, const)],
        out_specs=pl.BlockSpec((tm, D_MODEL), row),
        out_shape=jax.ShapeDtypeStruct((t, D_MODEL), F32),
        compiler_params=pltpu.CompilerParams(
            dimension_semantics=("arbitrary",), vmem_limit_bytes=VMEM_LIMIT),
        name="oproj",
    )(ma, mb, x2, w)


def _ffn_kernel(nseq, final, hp_ref, h_ref, hn_ref, gf_ref, wg_ref, wu_ref, cw_ref, cb_ref, wd_ref, gl_ref,
                o_ref, xn_ref, acc_ref):
    i = pl.program_id(0)
    tm = h_ref.shape[0]
    hal = FFN_HALO
    gf = gf_ref[...]
    first = (i % nseq) == 0
    last = (i % nseq) == nseq - 1
    xn_ref[0:hal, :] = jnp.where(first, 0.0, _rms(hp_ref[...], gf)).astype(BF16)
    xn_ref[hal:hal + tm, :] = _rms(h_ref[...], gf).astype(BF16)
    xn_ref[hal + tm:, :] = jnp.where(last, 0.0, _rms(hn_ref[...], gf)).astype(BF16)
    rows = tm + 2 * hal

    for n, (lo, hi) in enumerate(FFN_CHUNKS):
        g = jnp.dot(xn_ref[...], wg_ref[:, lo:hi], preferred_element_type=F32)
        gc = (pltpu.roll(g, 1, 0)[hal:hal + tm] * cw_ref[0:1, lo:hi]
              + g[hal:hal + tm] * cw_ref[1:2, lo:hi]
              + pltpu.roll(g, rows - 1, 0)[hal:hal + tm] * cw_ref[2:3, lo:hi]
              + cb_ref[:, lo:hi])
        u = jnp.dot(xn_ref[hal:hal + tm, :], wu_ref[:, lo:hi], preferred_element_type=F32)
        act = (gc / (1.0 + jnp.exp(-gc)) * u).astype(BF16)
        part = jnp.dot(act, wd_ref[lo:hi, :], preferred_element_type=F32)
        if n == 0:
            acc_ref[...] = part
        else:
            acc_ref[...] += part

    out = h_ref[...] + acc_ref[...]
    o_ref[...] = _rms(out, gl_ref[...]) if final else out


def _ffn(h, gf, wg, wu, cw, cb, wd, gl, seq, final):
    t = h.shape[0]
    tm = FFN_TM
    nseq = seq // tm
    r = tm // FFN_HALO
    nhal = t // FFN_HALO
    row = lambda i: (i, 0)
    prev = lambda i: (jnp.maximum(i * r - 1, 0), 0)
    nxt = lambda i: (jnp.minimum((i + 1) * r, nhal - 1), 0)
    const = lambda i: (0, 0)
    once = lambda shape: pl.BlockSpec(shape, const, pipeline_mode=pl.Buffered(1))
    return pl.pallas_call(
        functools.partial(_ffn_kernel, nseq, final),
        grid=(t // tm,),
        in_specs=[pl.BlockSpec((FFN_HALO, D_MODEL), prev),
                  pl.BlockSpec((tm, D_MODEL), row),
                  pl.BlockSpec((FFN_HALO, D_MODEL), nxt),
                  once((1, D_MODEL)), once(wg.shape), once(wu.shape), once(cw.shape),
                  once(cb.shape), once(wd.shape), once((1, D_MODEL))],
        out_specs=pl.BlockSpec((tm, D_MODEL), row),
        out_shape=jax.ShapeDtypeStruct((t, D_MODEL), F32),
        scratch_shapes=[pltpu.VMEM((tm + 2 * FFN_HALO, D_MODEL), BF16),
                        pltpu.VMEM((tm, D_MODEL), F32)],
        compiler_params=pltpu.CompilerParams(
            dimension_semantics=("arbitrary",), vmem_limit_bytes=VMEM_LIMIT),
        name="ffn",
    )(h, h, h, gf, wg, wu, cw, cb, wd, gl)


def _rope_tables(seq):
    pos = jnp.arange(seq, dtype=F32)
    inv_freq = ROPE_THETA ** (-jnp.arange(0, ROT_DIM, 2, dtype=F32) / ROT_DIM)
    ang = pos[:, None] * inv_freq[None, :]
    cos = jnp.cos(ang)
    sin = jnp.sin(ang)
    half = ROT_DIM // 2
    ones = jnp.ones((seq, HEAD_DIM - ROT_DIM), F32)
    zeros = jnp.zeros((seq, HEAD_DIM - ROT_DIM), F32)
    zh = jnp.zeros((seq, half), F32)
    c = jnp.concatenate([cos, cos, ones], axis=1)
    s1 = jnp.concatenate([zh, sin, zeros], axis=1)
    s2 = jnp.concatenate([-sin, zh, zeros], axis=1)
    rep = LANES // HEAD_DIM
    return tuple(jnp.tile(t, (1, rep)) for t in (c, s1, s2))


def kernel(x, norm_mix, w_in, sink, rpb, norm_a, norm_b, w_out, norm_ffn,
           w_gate, w_up, conv_w, conv_b, w_down, norm_final):
    batch, seq, d = x.shape
    depth = w_in.shape[0]
    assert d == D_MODEL and seq % PROJ_TM == 0 and seq % FFN_TM == 0 and seq % (A_SUB * BLK) == 0
    assert seq % NB_TOK == 0 and seq // GRID_W >= 3 * NB_ROWS
    c, s1, s2 = _rope_tables(seq)
    h = x.reshape(batch * seq, d)
    for l in range(depth):
        qa, kva, qbl, qbr, kb, vb = _proj(h, norm_mix[l][None], w_in[l].astype(BF16), c, s1, s2, seq)
        ma = _attn_a(sink[l].astype(F32), qa, kva, norm_a[l][None].astype(F32), batch, seq)
        mb = _attn_b(qbl, qbr, kb, vb, _nb_bias(rpb[l]), norm_b[l][None].astype(F32), batch, seq)
        h = _oproj(ma, mb, h, w_out[l].astype(BF16))
        h = _ffn(h, norm_ffn[l][None], w_gate[l].astype(BF16), w_up[l].astype(BF16),
                 conv_w[l], conv_b[l][None], w_down[l].astype(BF16), norm_final[None], seq,
                 final=(l == depth - 1))
    return h.reshape(batch, seq, d)
```

```python
import functools
import math

import numpy as np
import jax
import jax.numpy as jnp
from jax import lax
from jax.experimental import pallas as pl
from jax.experimental.pallas import tpu as pltpu

D_MODEL = 1024
HEAD_DIM = 64
A_HEADS = 8
A_KV_HEADS = 2
B_HEADS = 8
A_Q_W = A_HEADS * HEAD_DIM
A_KV_W = A_KV_HEADS * HEAD_DIM
B_W = B_HEADS * HEAD_DIM
WINDOW = 128
BLK = 128
ROPE_THETA = 500000.0
ROT_DIM = HEAD_DIM // 4
GRID_W = 64
NA_KH = 8
NA_KW = 16
D_FF = 2816
EPS = 1e-6
NEG = -1e30

LANES = 128
VMEM_LIMIT = 56 * 1024 * 1024

F32 = jnp.float32
BF16 = jnp.bfloat16

PROJ_TM = 512
A_SUB = 4
B_SUB = 2
FFN_TM = 512
FFN_TF = 512
NB_ROWS = 4
NB_TOK = NB_ROWS * GRID_W
NB_WIN = 3 * NB_TOK
FFN_HALO = 16
FFN_CHUNKS = tuple((lo, min(lo + FFN_TF, D_FF)) for lo in range(0, D_FF, FFN_TF))


def _rms(x, g):
    return x * lax.rsqrt(jnp.mean(x * x, axis=-1, keepdims=True) + EPS) * g


def _proj_kernel(x_ref, g_ref, w_ref, c_ref, s1_ref, s2_ref,
                 qa_ref, kva_ref, qbl_ref, qbr_ref, kb_ref, vb_ref):
    hn = _rms(x_ref[...], g_ref[...]).astype(BF16)
    c = c_ref[...]
    s1 = s1_ref[...]
    s2 = s2_ref[...]
    tm = hn.shape[0]
    left = lax.broadcasted_iota(jnp.int32, (tm, LANES), 1) < HEAD_DIM
    scale = 1.0 / math.sqrt(HEAD_DIM)

    def rope(t):
        return t * c + pltpu.roll(t, 8, 1) * s1 + pltpu.roll(t, LANES - 8, 1) * s2

    def mm(lo, hi):
        return jnp.dot(hn, w_ref[:, lo:hi], preferred_element_type=F32)

    qa = mm(0, A_Q_W)
    for j in range(A_Q_W // LANES):
        sl = slice(j * LANES, (j + 1) * LANES)
        qa_ref[:, sl] = (rope(qa[:, sl]) * scale).astype(BF16)

    kv = mm(A_Q_W, A_Q_W + 2 * A_KV_W)
    ka = rope(kv[:, :LANES])
    va = kv[:, LANES:]
    zero = jnp.zeros_like(ka)
    for base, t in ((0, ka), (4, va)):
        sw = pltpu.roll(t, HEAD_DIM, 1)
        parts = (jnp.where(left, t, zero), jnp.where(left, zero, sw),
                 jnp.where(left, sw, zero), jnp.where(left, zero, t))
        for n, p in enumerate(parts):
            kva_ref[:, (base + n) * LANES:(base + n + 1) * LANES] = p.astype(BF16)

    o = A_Q_W + 2 * A_KV_W
    qb = mm(o, o + B_W) * scale
    for j in range(B_W // LANES):
        sl = slice(j * LANES, (j + 1) * LANES)
        t = qb[:, sl]
        qbl_ref[:, sl] = jnp.where(left, t, zero).astype(BF16)
        qbr_ref[:, sl] = jnp.where(left, zero, t).astype(BF16)
    kb_ref[...] = mm(o + B_W, o + 2 * B_W).astype(BF16)
    vb_ref[...] = mm(o + 2 * B_W, o + 3 * B_W).astype(BF16)


def _proj(x2, g, w, c, s1, s2, seq):
    t = x2.shape[0]
    tm = PROJ_TM
    nseq = seq // tm
    row = lambda i: (i, 0)
    const = lambda i: (0, 0)
    tab = lambda i: (i % nseq, 0)
    outs = [(A_Q_W, row), (8 * LANES, row), (B_W, row), (B_W, row), (B_W, row), (B_W, row)]
    return pl.pallas_call(
        _proj_kernel,
        grid=(t // tm,),
        in_specs=[pl.BlockSpec((tm, D_MODEL), row),
                  pl.BlockSpec((1, D_MODEL), const),
                  pl.BlockSpec(w.shape, const),
                  pl.BlockSpec((tm, LANES), tab),
                  pl.BlockSpec((tm, LANES), tab),
                  pl.BlockSpec((tm, LANES), tab)],
        out_specs=[pl.BlockSpec((tm, wd), im) for wd, im in outs],
        out_shape=[jax.ShapeDtypeStruct((t, wd), BF16) for wd, _ in outs],
        compiler_params=pltpu.CompilerParams(
            dimension_semantics=("arbitrary",), vmem_limit_bytes=VMEM_LIMIT),
        name="proj",
    )(x2, g, w, c, s1, s2)


def _attn_a_kernel(sink_ref, q_ref, kp_ref, kc_ref, kn_ref, g_ref, o_ref):
    i = pl.program_id(1)
    nstep = pl.num_programs(1)
    span = BLK + 2 * WINDOW
    a = lax.broadcasted_iota(jnp.int32, (BLK, span), 0)
    b = lax.broadcasted_iota(jnp.int32, (BLK, span), 1)
    d = b - a
    band = (d >= 0) & (d <= 2 * WINDOW)
    left = lax.broadcasted_iota(jnp.int32, (BLK, LANES), 1) < HEAD_DIM

    def key_block(idx, cols):
        if idx == 0:
            return kp_ref[:, cols]
        if idx <= A_SUB:
            return kc_ref[(idx - 1) * BLK:idx * BLK, cols]
        return kn_ref[:, cols]

    for j in range(A_SUB):
        valid = band
        if j == 0:
            valid = valid & (b >= jnp.where(i == 0, WINDOW, 0))
        if j == A_SUB - 1:
            valid = valid & (b < jnp.where(i == nstep - 1, WINDOW + BLK, span))
        rows = slice(j * BLK, (j + 1) * BLK)
        outs = []
        for g in range(A_KV_HEADS):
            def stack(col):
                return jnp.concatenate(
                    [key_block(j + w, slice((col + n) * LANES, (col + n + 1) * LANES))
                     for n in range(2) for w in range(3)], axis=0)
            kbd = stack(2 * g)
            vbd = stack(4 + 2 * g)
            q2 = jnp.concatenate(
                [q_ref[rows, (2 * g + p) * LANES:(2 * g + p + 1) * LANES] for p in range(2)], axis=0)
            s = lax.dot_general(q2, kbd, (((1,), (1,)), ((), ())), preferred_element_type=F32)
            prow = []
            dens = []
            for p in range(2):
                pcol = []
                dpair = []
                for hh in range(2):
                    sk = sink_ref[4 * g + 2 * p + hh]
                    sq = jnp.where(valid, s[p * BLK:(p + 1) * BLK, hh * span:(hh + 1) * span], NEG)
                    m = jnp.maximum(jnp.max(sq, axis=-1, keepdims=True), sk)
                    e = jnp.exp(sq - m)
                    dpair.append(jnp.sum(e, axis=-1, keepdims=True) + jnp.exp(sk - m))
                    pcol.append(e.astype(BF16))
                prow.append(jnp.concatenate(pcol, axis=1))
                dens.append(jnp.where(left, dpair[0], dpair[1]))
            pm = jnp.concatenate(prow, axis=0)
            o2 = jnp.dot(pm, vbd, preferred_element_type=F32)
            for p in range(2):
                outs.append(o2[p * BLK:(p + 1) * BLK] / dens[p])
        o = jnp.concatenate(outs, axis=1)
        o_ref[rows, :] = _rms(o, g_ref[...]).astype(BF16)


def _attn_a(sink, qa, kva, g, batch, seq):
    t = qa.shape[0]
    nblk = seq // BLK
    nstep = nblk // A_SUB
    cur = lambda b, i: (b * nstep + i, 0)
    prev = lambda b, i: (b * nblk + jnp.maximum(i * A_SUB - 1, 0), 0)
    nxt = lambda b, i: (b * nblk + jnp.minimum((i + 1) * A_SUB, nblk - 1), 0)
    const = lambda b, i: (0, 0)
    kw = kva.shape[1]
    return pl.pallas_call(
        _attn_a_kernel,
        grid=(batch, nstep),
        in_specs=[pl.BlockSpec(memory_space=pltpu.SMEM),
                  pl.BlockSpec((A_SUB * BLK, A_Q_W), cur),
                  pl.BlockSpec((BLK, kw), prev),
                  pl.BlockSpec((A_SUB * BLK, kw), cur),
                  pl.BlockSpec((BLK, kw), nxt),
                  pl.BlockSpec((1, A_Q_W), const)],
        out_specs=pl.BlockSpec((A_SUB * BLK, A_Q_W), cur),
        out_shape=jax.ShapeDtypeStruct((t, A_Q_W), BF16),
        compiler_params=pltpu.CompilerParams(
            dimension_semantics=("arbitrary", "arbitrary"), vmem_limit_bytes=VMEM_LIMIT),
        name="attn_a",
    )(sink, qa, kva, kva, kva, g)


def _attn_b_kernel(ql_ref, qr_ref, kp_ref, kc_ref, kn_ref, vp_ref, vc_ref, vn_ref,
                   bias_ref, g_ref, o_ref):
    js = pl.program_id(1)
    ngrp = pl.num_programs(1) * B_SUB
    left = lax.broadcasted_iota(jnp.int32, (NB_TOK, LANES), 1) < HEAD_DIM

    def win_block(refs, idx, cols):
        p_ref, c_ref, n_ref = refs
        if idx == 0:
            return p_ref[:, cols]
        if idx <= B_SUB:
            return c_ref[(idx - 1) * NB_TOK:idx * NB_TOK, cols]
        return n_ref[:, cols]

    for j in range(B_SUB):
        grp = js * B_SUB + j
        cfg = jnp.where(grp == 0, 0, jnp.where(grp == ngrp - 1, 2, 1))
        rows = slice(j * NB_TOK, (j + 1) * NB_TOK)
        outs = []
        for p in range(B_HEADS // 2):
            sl = slice(p * LANES, (p + 1) * LANES)
            q2 = jnp.concatenate([ql_ref[rows, sl], qr_ref[rows, sl]], axis=0)
            kw = jnp.concatenate([win_block((kp_ref, kc_ref, kn_ref), j + w, sl) for w in range(3)], axis=0)
            vw = jnp.concatenate([win_block((vp_ref, vc_ref, vn_ref), j + w, sl) for w in range(3)], axis=0)
            s = lax.dot_general(q2, kw, (((1,), (1,)), ((), ())), preferred_element_type=F32)
            ps = []
            ls = []
            for hh in range(2):
                sq = s[hh * NB_TOK:(hh + 1) * NB_TOK] + bias_ref[cfg, 2 * p + hh]
                m = jnp.max(sq, axis=-1, keepdims=True)
                e = jnp.exp(sq - m)
                ls.append(jnp.sum(e, axis=-1, keepdims=True))
                ps.append(e.astype(BF16))
            o2 = jnp.dot(jnp.concatenate(ps, axis=0), vw, preferred_element_type=F32)
            outs.append(jnp.where(left, o2[:NB_TOK] / ls[0], o2[NB_TOK:] / ls[1]))
        o = jnp.concatenate(outs, axis=1)
        o_ref[rows, :] = _rms(o, g_ref[...]).astype(BF16)


def _attn_b(qbl, qbr, kb, vb, bias, g, batch, seq):
    t = qbl.shape[0]
    ngrp = seq // NB_TOK
    nstep = ngrp // B_SUB
    cur = lambda b, j: (b * nstep + j, 0)
    prev = lambda b, j: (b * ngrp + jnp.maximum(j * B_SUB - 1, 0), 0)
    nxt = lambda b, j: (b * ngrp + jnp.minimum((j + 1) * B_SUB, ngrp - 1), 0)
    const = lambda b, j: (0, 0)
    blk = lambda im: pl.BlockSpec((NB_TOK, B_W), im)
    big = pl.BlockSpec((B_SUB * NB_TOK, B_W), cur)
    return pl.pallas_call(
        _attn_b_kernel,
        grid=(batch, nstep),
        in_specs=[big, big, blk(prev), big, blk(nxt), blk(prev), big, blk(nxt),
                  pl.BlockSpec(bias.shape, lambda b, j: (0, 0, 0, 0), pipeline_mode=pl.Buffered(1)),
                  pl.BlockSpec((1, B_W), const)],
        out_specs=big,
        out_shape=jax.ShapeDtypeStruct((t, B_W), BF16),
        compiler_params=pltpu.CompilerParams(
            dimension_semantics=("arbitrary", "arbitrary"), vmem_limit_bytes=VMEM_LIMIT),
        name="attn_b",
    )(qbl, qbr, kb, kb, kb, vb, vb, vb, bias, g)


def _nb_bias(rpb_l):
    rows, n = NB_ROWS, GRID_W
    r = rpb_l.astype(F32)
    lead = r.shape[:-1]
    pad = n - NA_KW
    v = jnp.concatenate([jnp.broadcast_to(r[..., :1], lead + (pad,)), r,
                         jnp.broadcast_to(r[..., -1:], lead + (pad,))], axis=-1)
    w = jnp.concatenate([v[..., n - 1:], v[..., :1], v[..., :n - 1]], axis=-1)
    t = jnp.tile(w, (1, 1, n))[..., :n * (2 * n - 1)].reshape(lead + (n, 2 * n - 1))[..., :n]
    cq = np.arange(n)[:, None]
    kc = np.arange(n)[None, :]
    cs = np.clip(cq - NA_KW // 2, 0, n - NA_KW)
    t = jnp.where(jnp.asarray((kc >= cs) & (kc < cs + NA_KW)), t, NEG)
    t = t.transpose(0, 2, 1, 3)
    off = NA_KH - 1 - rows
    blocks = jnp.stack([t[:, :, off - qr:off - qr + 3 * rows] for qr in range(rows)], axis=1)
    qr = np.arange(rows)[:, None]
    wr = np.arange(3 * rows)[None, :]
    row_ok = np.stack([np.broadcast_to(wr >= rows, (rows, 3 * rows)),
                       (wr - qr >= 0) & (wr - qr < NA_KH),
                       np.broadcast_to(wr < NA_KH, (rows, 3 * rows))])
    full = jnp.where(jnp.asarray(row_ok)[:, None, :, None, :, None], blocks[None], NEG)
    return full.reshape(3, B_HEADS, NB_TOK, NB_WIN)


def _ffn_kernel(nseq, final, ap_ref, a_ref, an_ref, bp_ref, b_ref, bn_ref, xp_ref, x_ref, xx_ref,
                wo_ref, gf_ref, wg_ref, wu_ref, cw_ref, cb_ref, wd_ref, gl_ref,
                o_ref, xn_ref, h_ref, act_ref):
    i = pl.program_id(0)
    tm = x_ref.shape[0]
    hal = FFN_HALO
    rows = tm + 2 * hal
    mix = jnp.concatenate(
        [jnp.concatenate([ap_ref[...], a_ref[...], an_ref[...]], axis=0),
         jnp.concatenate([bp_ref[...], b_ref[...], bn_ref[...]], axis=0)], axis=1)
    xc = jnp.concatenate([xp_ref[...], x_ref[...], xx_ref[...]], axis=0)
    h = xc + jnp.dot(mix, wo_ref[...], preferred_element_type=F32)
    h_ref[...] = h[hal:hal + tm]
    first = (i % nseq) == 0
    last = (i % nseq) == nseq - 1
    r = lax.broadcasted_iota(jnp.int32, (rows, 1), 0)
    outside = (first & (r < hal)) | (last & (r >= hal + tm))
    xn_ref[...] = jnp.where(outside, 0.0, _rms(h, gf_ref[...])).astype(BF16)

    for lo, hi in FFN_CHUNKS:
        g = jnp.dot(xn_ref[...], wg_ref[:, lo:hi], preferred_element_type=F32)
        gc = (pltpu.roll(g, 1, 0)[hal:hal + tm] * cw_ref[0:1, lo:hi]
              + g[hal:hal + tm] * cw_ref[1:2, lo:hi]
              + pltpu.roll(g, rows - 1, 0)[hal:hal + tm] * cw_ref[2:3, lo:hi]
              + cb_ref[:, lo:hi])
        u = jnp.dot(xn_ref[hal:hal + tm, :], wu_ref[:, lo:hi], preferred_element_type=F32)
        act_ref[:, lo:hi] = (gc / (1.0 + jnp.exp(-gc)) * u).astype(BF16)

    out = h_ref[...] + jnp.dot(act_ref[...], wd_ref[...], preferred_element_type=F32)
    o_ref[...] = _rms(out, gl_ref[...]) if final else out


def _ffn(ma, mb, x2, wo, gf, wg, wu, cw, cb, wd, gl, seq, final):
    t = x2.shape[0]
    tm = FFN_TM
    nseq = seq // tm
    r = tm // FFN_HALO
    nhal = t // FFN_HALO
    row = lambda i: (i, 0)
    prev = lambda i: (jnp.maximum(i * r - 1, 0), 0)
    nxt = lambda i: (jnp.minimum((i + 1) * r, nhal - 1), 0)
    const = lambda i: (0, 0)
    once = lambda shape: pl.BlockSpec(shape, const, pipeline_mode=pl.Buffered(1))
    halo3 = lambda w: [pl.BlockSpec((FFN_HALO, w), prev), pl.BlockSpec((tm, w), row),
                       pl.BlockSpec((FFN_HALO, w), nxt)]
    return pl.pallas_call(
        functools.partial(_ffn_kernel, nseq, final),
        grid=(t // tm,),
        in_specs=halo3(A_Q_W) + halo3(B_W) + halo3(D_MODEL) + [
            once(wo.shape), once((1, D_MODEL)), once(wg.shape), once(wu.shape), once(cw.shape),
            once(cb.shape), once(wd.shape), once((1, D_MODEL))],
        out_specs=pl.BlockSpec((tm, D_MODEL), row),
        out_shape=jax.ShapeDtypeStruct((t, D_MODEL), F32),
        scratch_shapes=[pltpu.VMEM((tm + 2 * FFN_HALO, D_MODEL), BF16),
                        pltpu.VMEM((tm, D_MODEL), F32),
                        pltpu.VMEM((tm, D_FF), BF16)],
        compiler_params=pltpu.CompilerParams(
            dimension_semantics=("arbitrary",), vmem_limit_bytes=VMEM_LIMIT),
        name="ffn",
    )(ma, ma, ma, mb, mb, mb, x2, x2, x2, wo, gf, wg, wu, cw, cb, wd, gl)


def _rope_tables(seq):
    pos = jnp.arange(seq, dtype=F32)
    inv_freq = ROPE_THETA ** (-jnp.arange(0, ROT_DIM, 2, dtype=F32) / ROT_DIM)
    ang = pos[:, None] * inv_freq[None, :]
    cos = jnp.cos(ang)
    sin = jnp.sin(ang)
    half = ROT_DIM // 2
    ones = jnp.ones((seq, HEAD_DIM - ROT_DIM), F32)
    zeros = jnp.zeros((seq, HEAD_DIM - ROT_DIM), F32)
    zh = jnp.zeros((seq, half), F32)
    c = jnp.concatenate([cos, cos, ones], axis=1)
    s1 = jnp.concatenate([zh, sin, zeros], axis=1)
    s2 = jnp.concatenate([-sin, zh, zeros], axis=1)
    rep = LANES // HEAD_DIM
    return tuple(jnp.tile(t, (1, rep)) for t in (c, s1, s2))


def kernel(x, norm_mix, w_in, sink, rpb, norm_a, norm_b, w_out, norm_ffn,
           w_gate, w_up, conv_w, conv_b, w_down, norm_final):
    batch, seq, d = x.shape
    depth = w_in.shape[0]
    assert d == D_MODEL and seq % PROJ_TM == 0 and seq % FFN_TM == 0 and seq % (A_SUB * BLK) == 0
    assert seq % (B_SUB * NB_TOK) == 0 and seq // GRID_W >= 3 * NB_ROWS
    c, s1, s2 = _rope_tables(seq)
    h = x.reshape(batch * seq, d)
    for l in range(depth):
        qa, kva, qbl, qbr, kb, vb = _proj(h, norm_mix[l][None], w_in[l].astype(BF16), c, s1, s2, seq)
        ma = _attn_a(sink[l].astype(F32), qa, kva, norm_a[l][None].astype(F32), batch, seq)
        mb = _attn_b(qbl, qbr, kb, vb, _nb_bias(rpb[l]), norm_b[l][None].astype(F32), batch, seq)
        h = _ffn(ma, mb, h, w_out[l].astype(BF16), norm_ffn[l][None], w_gate[l].astype(BF16), w_up[l].astype(BF16),
                 conv_w[l], conv_b[l][None], w_down[l].astype(BF16), norm_final[None], seq,
                 final=(l == depth - 1))
    return h.reshape(batch, seq, d)
```

```python
import functools
import math

import numpy as np
import jax
import jax.numpy as jnp
from jax import lax
from jax.experimental import pallas as pl
from jax.experimental.pallas import tpu as pltpu

D_MODEL = 1024
HEAD_DIM = 64
A_HEADS = 8
A_KV_HEADS = 2
B_HEADS = 8
A_Q_W = A_HEADS * HEAD_DIM
A_KV_W = A_KV_HEADS * HEAD_DIM
B_W = B_HEADS * HEAD_DIM
WINDOW = 128
BLK = 128
ROPE_THETA = 500000.0
ROT_DIM = HEAD_DIM // 4
GRID_W = 64
NA_KH = 8
NA_KW = 16
D_FF = 2816
EPS = 1e-6
NEG = -1e30
LOG2E = math.log2(math.e)

LANES = 128
VMEM_LIMIT = 56 * 1024 * 1024

F32 = jnp.float32
BF16 = jnp.bfloat16

PROJ_TM = 512
A_SUB = 4
B_SUB = 2
FFN_TM = 512
FFN_TF = 512
NB_ROWS = 4
NB_TOK = NB_ROWS * GRID_W
NB_WIN = 3 * NB_TOK
FFN_HALO = 16
FFN_CHUNKS = tuple((lo, min(lo + FFN_TF, D_FF)) for lo in range(0, D_FF, FFN_TF))


def _rms(x, g):
    return x * lax.rsqrt(jnp.mean(x * x, axis=-1, keepdims=True) + EPS) * g


def _proj_kernel(x_ref, g_ref, w_ref, c_ref, s1_ref, s2_ref,
                 qa_ref, kva_ref, qbl_ref, qbr_ref, kb_ref, vb_ref):
    hn = _rms(x_ref[...], g_ref[...]).astype(BF16)
    c = c_ref[...]
    s1 = s1_ref[...]
    s2 = s2_ref[...]
    tm = hn.shape[0]
    left = lax.broadcasted_iota(jnp.int32, (tm, LANES), 1) < HEAD_DIM
    scale = LOG2E / math.sqrt(HEAD_DIM)

    def rope(t):
        return t * c + pltpu.roll(t, 8, 1) * s1 + pltpu.roll(t, LANES - 8, 1) * s2

    def mm(lo, hi):
        return jnp.dot(hn, w_ref[:, lo:hi], preferred_element_type=F32)

    qa = mm(0, A_Q_W)
    for j in range(A_Q_W // LANES):
        sl = slice(j * LANES, (j + 1) * LANES)
        qa_ref[:, sl] = (rope(qa[:, sl]) * scale).astype(BF16)

    kv = mm(A_Q_W, A_Q_W + 2 * A_KV_W)
    ka = rope(kv[:, :LANES])
    va = kv[:, LANES:]
    zero = jnp.zeros_like(ka)
    for base, t in ((0, ka), (4, va)):
        sw = pltpu.roll(t, HEAD_DIM, 1)
        parts = (jnp.where(left, t, zero), jnp.where(left, zero, sw),
                 jnp.where(left, sw, zero), jnp.where(left, zero, t))
        for n, p in enumerate(parts):
            kva_ref[:, (base + n) * LANES:(base + n + 1) * LANES] = p.astype(BF16)

    o = A_Q_W + 2 * A_KV_W
    qb = mm(o, o + B_W) * scale
    for j in range(B_W // LANES):
        sl = slice(j * LANES, (j + 1) * LANES)
        t = qb[:, sl]
        qbl_ref[:, sl] = jnp.where(left, t, zero).astype(BF16)
        qbr_ref[:, sl] = jnp.where(left, zero, t).astype(BF16)
    kb_ref[...] = mm(o + B_W, o + 2 * B_W).astype(BF16)
    vb_ref[...] = mm(o + 2 * B_W, o + 3 * B_W).astype(BF16)


def _proj(x2, g, w, c, s1, s2, seq):
    t = x2.shape[0]
    tm = PROJ_TM
    nseq = seq // tm
    row = lambda i: (i, 0)
    const = lambda i: (0, 0)
    tab = lambda i: (i % nseq, 0)
    outs = [(A_Q_W, row), (8 * LANES, row), (B_W, row), (B_W, row), (B_W, row), (B_W, row)]
    return pl.pallas_call(
        _proj_kernel,
        grid=(t // tm,),
        in_specs=[pl.BlockSpec((tm, D_MODEL), row),
                  pl.BlockSpec((1, D_MODEL), const),
                  pl.BlockSpec(w.shape, const),
                  pl.BlockSpec((tm, LANES), tab),
                  pl.BlockSpec((tm, LANES), tab),
                  pl.BlockSpec((tm, LANES), tab)],
        out_specs=[pl.BlockSpec((tm, wd), im) for wd, im in outs],
        out_shape=[jax.ShapeDtypeStruct((t, wd), BF16) for wd, _ in outs],
        compiler_params=pltpu.CompilerParams(
            dimension_semantics=("arbitrary",), vmem_limit_bytes=VMEM_LIMIT),
        name="proj",
    )(x2, g, w, c, s1, s2)


def _attn_a_kernel(sink_ref, q_ref, kp_ref, kc_ref, kn_ref, g_ref, o_ref):
    i = pl.program_id(1)
    nstep = pl.num_programs(1)
    span = BLK + 2 * WINDOW
    a = lax.broadcasted_iota(jnp.int32, (BLK, span), 0)
    b = lax.broadcasted_iota(jnp.int32, (BLK, span), 1)
    d = b - a
    band = (d >= 0) & (d <= 2 * WINDOW)
    left = lax.broadcasted_iota(jnp.int32, (BLK, LANES), 1) < HEAD_DIM

    def key_block(idx, cols):
        if idx == 0:
            return kp_ref[:, cols]
        if idx <= A_SUB:
            return kc_ref[(idx - 1) * BLK:idx * BLK, cols]
        return kn_ref[:, cols]

    for j in range(A_SUB):
        valid = band
        if j == 0:
            valid = valid & (b >= jnp.where(i == 0, WINDOW, 0))
        if j == A_SUB - 1:
            valid = valid & (b < jnp.where(i == nstep - 1, WINDOW + BLK, span))
        rows = slice(j * BLK, (j + 1) * BLK)
        outs = []
        for g in range(A_KV_HEADS):
            def stack(col):
                return jnp.concatenate(
                    [key_block(j + w, slice((col + n) * LANES, (col + n + 1) * LANES))
                     for n in range(2) for w in range(3)], axis=0)
            kbd = stack(2 * g)
            vbd = stack(4 + 2 * g)
            q2 = jnp.concatenate(
                [q_ref[rows, (2 * g + p) * LANES:(2 * g + p + 1) * LANES] for p in range(2)], axis=0)
            s = lax.dot_general(q2, kbd, (((1,), (1,)), ((), ())), preferred_element_type=F32)
            prow = []
            dens = []
            for p in range(2):
                pcol = []
                dpair = []
                for hh in range(2):
                    sk = sink_ref[4 * g + 2 * p + hh]
                    sq = jnp.where(valid, s[p * BLK:(p + 1) * BLK, hh * span:(hh + 1) * span], NEG)
                    m = jnp.maximum(jnp.max(sq, axis=-1, keepdims=True), sk)
                    e = jnp.exp2(sq - m)
                    dpair.append(jnp.sum(e, axis=-1, keepdims=True) + jnp.exp2(sk - m))
                    pcol.append(e.astype(BF16))
                prow.append(jnp.concatenate(pcol, axis=1))
                dens.append(jnp.where(left, dpair[0], dpair[1]))
            pm = jnp.concatenate(prow, axis=0)
            o2 = jnp.dot(pm, vbd, preferred_element_type=F32)
            for p in range(2):
                outs.append(o2[p * BLK:(p + 1) * BLK] / dens[p])
        o = jnp.concatenate(outs, axis=1)
        o_ref[rows, :] = _rms(o, g_ref[...]).astype(BF16)


def _attn_a(sink, qa, kva, g, batch, seq):
    t = qa.shape[0]
    nblk = seq // BLK
    nstep = nblk // A_SUB
    cur = lambda b, i: (b * nstep + i, 0)
    prev = lambda b, i: (b * nblk + jnp.maximum(i * A_SUB - 1, 0), 0)
    nxt = lambda b, i: (b * nblk + jnp.minimum((i + 1) * A_SUB, nblk - 1), 0)
    const = lambda b, i: (0, 0)
    kw = kva.shape[1]
    return pl.pallas_call(
        _attn_a_kernel,
        grid=(batch, nstep),
        in_specs=[pl.BlockSpec(memory_space=pltpu.SMEM),
                  pl.BlockSpec((A_SUB * BLK, A_Q_W), cur),
                  pl.BlockSpec((BLK, kw), prev),
                  pl.BlockSpec((A_SUB * BLK, kw), cur),
                  pl.BlockSpec((BLK, kw), nxt),
                  pl.BlockSpec((1, A_Q_W), const)],
        out_specs=pl.BlockSpec((A_SUB * BLK, A_Q_W), cur),
        out_shape=jax.ShapeDtypeStruct((t, A_Q_W), BF16),
        compiler_params=pltpu.CompilerParams(
            dimension_semantics=("arbitrary", "arbitrary"), vmem_limit_bytes=VMEM_LIMIT),
        name="attn_a",
    )(sink, qa, kva, kva, kva, g)


def _attn_b_kernel(ql_ref, qr_ref, kp_ref, kc_ref, kn_ref, vp_ref, vc_ref, vn_ref,
                   bias_ref, g_ref, o_ref):
    js = pl.program_id(1)
    ngrp = pl.num_programs(1) * B_SUB
    left = lax.broadcasted_iota(jnp.int32, (NB_TOK, LANES), 1) < HEAD_DIM

    def win_block(refs, idx, cols):
        p_ref, c_ref, n_ref = refs
        if idx == 0:
            return p_ref[:, cols]
        if idx <= B_SUB:
            return c_ref[(idx - 1) * NB_TOK:idx * NB_TOK, cols]
        return n_ref[:, cols]

    for j in range(B_SUB):
        grp = js * B_SUB + j
        cfg = jnp.where(grp == 0, 0, jnp.where(grp == ngrp - 1, 2, 1))
        rows = slice(j * NB_TOK, (j + 1) * NB_TOK)
        outs = []
        for p in range(B_HEADS // 2):
            sl = slice(p * LANES, (p + 1) * LANES)
            q2 = jnp.concatenate([ql_ref[rows, sl], qr_ref[rows, sl]], axis=0)
            kw = jnp.concatenate([win_block((kp_ref, kc_ref, kn_ref), j + w, sl) for w in range(3)], axis=0)
            vw = jnp.concatenate([win_block((vp_ref, vc_ref, vn_ref), j + w, sl) for w in range(3)], axis=0)
            s = lax.dot_general(q2, kw, (((1,), (1,)), ((), ())), preferred_element_type=F32)
            ps = []
            ls = []
            for hh in range(2):
                sq = s[hh * NB_TOK:(hh + 1) * NB_TOK] + bias_ref[cfg, 2 * p + hh]
                m = jnp.max(sq, axis=-1, keepdims=True)
                e = jnp.exp2(sq - m)
                ls.append(jnp.sum(e, axis=-1, keepdims=True))
                ps.append(e.astype(BF16))
            o2 = jnp.dot(jnp.concatenate(ps, axis=0), vw, preferred_element_type=F32)
            outs.append(jnp.where(left, o2[:NB_TOK] / ls[0], o2[NB_TOK:] / ls[1]))
        o = jnp.concatenate(outs, axis=1)
        o_ref[rows, :] = _rms(o, g_ref[...]).astype(BF16)


def _attn_b(qbl, qbr, kb, vb, bias, g, batch, seq):
    t = qbl.shape[0]
    ngrp = seq // NB_TOK
    nstep = ngrp // B_SUB
    cur = lambda b, j: (b * nstep + j, 0)
    prev = lambda b, j: (b * ngrp + jnp.maximum(j * B_SUB - 1, 0), 0)
    nxt = lambda b, j: (b * ngrp + jnp.minimum((j + 1) * B_SUB, ngrp - 1), 0)
    const = lambda b, j: (0, 0)
    blk = lambda im: pl.BlockSpec((NB_TOK, B_W), im)
    big = pl.BlockSpec((B_SUB * NB_TOK, B_W), cur)
    return pl.pallas_call(
        _attn_b_kernel,
        grid=(batch, nstep),
        in_specs=[big, big, blk(prev), big, blk(nxt), blk(prev), big, blk(nxt),
                  pl.BlockSpec(bias.shape, lambda b, j: (0, 0, 0, 0), pipeline_mode=pl.Buffered(1)),
                  pl.BlockSpec((1, B_W), const)],
        out_specs=big,
        out_shape=jax.ShapeDtypeStruct((t, B_W), BF16),
        compiler_params=pltpu.CompilerParams(
            dimension_semantics=("arbitrary", "arbitrary"), vmem_limit_bytes=VMEM_LIMIT),
        name="attn_b",
    )(qbl, qbr, kb, kb, kb, vb, vb, vb, bias, g)


def _nb_row_ok(cfg, qr, wr):
    if cfg == 0:
        return wr >= NB_ROWS
    if cfg == 2:
        return wr < NA_KH
    return 0 <= wr - qr < NA_KH


def _nb_bias_kernel(u_ref, o_ref):
    lane = lax.broadcasted_iota(jnp.int32, (GRID_W, LANES), 1)
    cq = lax.broadcasted_iota(jnp.int32, (GRID_W, LANES), 0)
    kc = lane & (GRID_W - 1)
    cs = jnp.clip(cq - NA_KW // 2, 0, GRID_W - NA_KW)
    col_ok = (kc >= cs) & (kc < cs + NA_KW)
    lo_half = lane < GRID_W
    neg = jnp.full((GRID_W, LANES), NEG, F32)
    off = NA_KH - 1 - NB_ROWS
    for qr in range(NB_ROWS):
        for k in range(NB_WIN // LANES):
            a = 2 * k - qr + off
            w = pltpu.roll(jnp.broadcast_to(u_ref[0, a:a + 1, :], (GRID_W, LANES)), 0, 1,
                           stride=1, stride_axis=0)
            for cfg in range(3):
                ok_lo = _nb_row_ok(cfg, qr, 2 * k)
                ok_hi = _nb_row_ok(cfg, qr, 2 * k + 1)
                if ok_lo and ok_hi:
                    blk = jnp.where(col_ok, w, neg)
                elif ok_lo:
                    blk = jnp.where(col_ok & lo_half, w, neg)
                elif ok_hi:
                    blk = jnp.where(col_ok & jnp.logical_not(lo_half), w, neg)
                else:
                    blk = neg
                o_ref[cfg, 0, qr * GRID_W:(qr + 1) * GRID_W, k * LANES:(k + 1) * LANES] = blk


def _nb_bias(rpb_l):
    r = rpb_l.astype(F32) * LOG2E
    heads, na, nb = r.shape
    half = nb // 2
    gap = (LANES - 2 * nb) // 2
    z = jnp.zeros((heads, na - 1, gap), F32)
    u = jnp.concatenate([r[:, :-1, half:], z, r[:, 1:, :], z, r[:, :-1, :half]], axis=-1)
    return pl.pallas_call(
        _nb_bias_kernel,
        grid=(heads,),
        in_specs=[pl.BlockSpec((1, na - 1, LANES), lambda h: (h, 0, 0))],
        out_specs=pl.BlockSpec((3, 1, NB_TOK, NB_WIN), lambda h: (0, h, 0, 0)),
        out_shape=jax.ShapeDtypeStruct((3, heads, NB_TOK, NB_WIN), F32),
        compiler_params=pltpu.CompilerParams(
            dimension_semantics=("arbitrary",), vmem_limit_bytes=VMEM_LIMIT),
        name="nb_bias",
    )(u)


def _ffn_kernel(nseq, final, ap_ref, a_ref, an_ref, bp_ref, b_ref, bn_ref, xp_ref, x_ref, xx_ref,
                wo_ref, gf_ref, wg_ref, wu_ref, cw_ref, cb_ref, wd_ref, gl_ref,
                o_ref, xn_ref, h_ref, act_ref):
    i = pl.program_id(0)
    tm = x_ref.shape[0]
    hal = FFN_HALO
    rows = tm + 2 * hal
    mix = jnp.concatenate(
        [jnp.concatenate([ap_ref[...], a_ref[...], an_ref[...]], axis=0),
         jnp.concatenate([bp_ref[...], b_ref[...], bn_ref[...]], axis=0)], axis=1)
    xc = jnp.concatenate([xp_ref[...], x_ref[...], xx_ref[...]], axis=0)
    h = xc + jnp.dot(mix, wo_ref[...], preferred_element_type=F32)
    h_ref[...] = h[hal:hal + tm]
    first = (i % nseq) == 0
    last = (i % nseq) == nseq - 1
    r = lax.broadcasted_iota(jnp.int32, (rows, 1), 0)
    outside = (first & (r < hal)) | (last & (r >= hal + tm))
    xn_ref[...] = jnp.where(outside, 0.0, _rms(h, gf_ref[...])).astype(BF16)

    for lo, hi in FFN_CHUNKS:
        g = jnp.dot(xn_ref[...], wg_ref[:, lo:hi], preferred_element_type=F32)
        gc = (pltpu.roll(g, 1, 0)[hal:hal + tm] * cw_ref[0:1, lo:hi]
              + g[hal:hal + tm] * cw_ref[1:2, lo:hi]
              + pltpu.roll(g, rows - 1, 0)[hal:hal + tm] * cw_ref[2:3, lo:hi]
              + cb_ref[:, lo:hi])
        u = jnp.dot(xn_ref[hal:hal + tm, :], wu_ref[:, lo:hi], preferred_element_type=F32)
        act_ref[:, lo:hi] = (gc / (1.0 + jnp.exp(-gc)) * u).astype(BF16)

    out = h_ref[...] + jnp.dot(act_ref[...], wd_ref[...], preferred_element_type=F32)
    o_ref[...] = _rms(out, gl_ref[...]) if final else out


def _ffn(ma, mb, x2, wo, gf, wg, wu, cw, cb, wd, gl, seq, final):
    t = x2.shape[0]
    tm = FFN_TM
    nseq = seq // tm
    r = tm // FFN_HALO
    nhal = t // FFN_HALO
    row = lambda i: (i, 0)
    prev = lambda i: (jnp.maximum(i * r - 1, 0), 0)
    nxt = lambda i: (jnp.minimum((i + 1) * r, nhal - 1), 0)
    const = lambda i: (0, 0)
    once = lambda shape: pl.BlockSpec(shape, const, pipeline_mode=pl.Buffered(1))
    halo3 = lambda w: [pl.BlockSpec((FFN_HALO, w), prev), pl.BlockSpec((tm, w), row),
                       pl.BlockSpec((FFN_HALO, w), nxt)]
    return pl.pallas_call(
        functools.partial(_ffn_kernel, nseq, final),
        grid=(t // tm,),
        in_specs=halo3(A_Q_W) + halo3(B_W) + halo3(D_MODEL) + [
            once(wo.shape), once((1, D_MODEL)), once(wg.shape), once(wu.shape), once(cw.shape),
            once(cb.shape), once(wd.shape), once((1, D_MODEL))],
        out_specs=pl.BlockSpec((tm, D_MODEL), row),
        out_shape=jax.ShapeDtypeStruct((t, D_MODEL), F32),
        scratch_shapes=[pltpu.VMEM((tm + 2 * FFN_HALO, D_MODEL), BF16),
                        pltpu.VMEM((tm, D_MODEL), F32),
                        pltpu.VMEM((tm, D_FF), BF16)],
        compiler_params=pltpu.CompilerParams(
            dimension_semantics=("arbitrary",), vmem_limit_bytes=VMEM_LIMIT),
        name="ffn",
    )(ma, ma, ma, mb, mb, mb, x2, x2, x2, wo, gf, wg, wu, cw, cb, wd, gl)


def _rope_tables(seq):
    f32 = np.float32
    pos = np.arange(seq, dtype=np.float64)
    inv_freq = ROPE_THETA ** (-np.arange(0, ROT_DIM, 2, dtype=np.float64) / ROT_DIM)
    ang = pos[:, None] * inv_freq[None, :]
    cos = np.cos(ang).astype(f32)
    sin = np.sin(ang).astype(f32)
    half = ROT_DIM // 2
    ones = np.ones((seq, HEAD_DIM - ROT_DIM), f32)
    zeros = np.zeros((seq, HEAD_DIM - ROT_DIM), f32)
    zh = np.zeros((seq, half), f32)
    c = np.concatenate([cos, cos, ones], axis=1)
    s1 = np.concatenate([zh, sin, zeros], axis=1)
    s2 = np.concatenate([-sin, zh, zeros], axis=1)
    rep = LANES // HEAD_DIM
    return tuple(jnp.asarray(np.tile(t, (1, rep))) for t in (c, s1, s2))


def kernel(x, norm_mix, w_in, sink, rpb, norm_a, norm_b, w_out, norm_ffn,
           w_gate, w_up, conv_w, conv_b, w_down, norm_final):
    batch, seq, d = x.shape
    depth = w_in.shape[0]
    assert d == D_MODEL and seq % PROJ_TM == 0 and seq % FFN_TM == 0 and seq % (A_SUB * BLK) == 0
    assert seq % (B_SUB * NB_TOK) == 0 and seq // GRID_W >= 3 * NB_ROWS
    c, s1, s2 = _rope_tables(seq)
    h = x.reshape(batch * seq, d)
    for l in range(depth):
        qa, kva, qbl, qbr, kb, vb = _proj(h, norm_mix[l][None], w_in[l].astype(BF16), c, s1, s2, seq)
        ma = _attn_a(sink[l].astype(F32) * LOG2E, qa, kva, norm_a[l][None].astype(F32), batch, seq)
        mb = _attn_b(qbl, qbr, kb, vb, _nb_bias(rpb[l]), norm_b[l][None].astype(F32), batch, seq)
        h = _ffn(ma, mb, h, w_out[l].astype(BF16), norm_ffn[l][None], w_gate[l].astype(BF16), w_up[l].astype(BF16),
                 conv_w[l], conv_b[l][None], w_down[l].astype(BF16), norm_final[None], seq,
                 final=(l == depth - 1))
    return h.reshape(batch, seq, d)
```

```python
import functools
import math

import numpy as np
import jax
import jax.numpy as jnp
from jax import lax
from jax.experimental import pallas as pl
from jax.experimental.pallas import tpu as pltpu

D_MODEL = 1024
HEAD_DIM = 64
A_HEADS = 8
A_KV_HEADS = 2
B_HEADS = 8
A_Q_W = A_HEADS * HEAD_DIM
A_KV_W = A_KV_HEADS * HEAD_DIM
B_W = B_HEADS * HEAD_DIM
WINDOW = 128
BLK = 128
ROPE_THETA = 500000.0
ROT_DIM = HEAD_DIM // 4
GRID_W = 64
NA_KH = 8
NA_KW = 16
D_FF = 2816
EPS = 1e-6
NEG = -1e30
LOG2E = math.log2(math.e)

LANES = 128
VMEM_LIMIT = 56 * 1024 * 1024

F32 = jnp.float32
BF16 = jnp.bfloat16

PROJ_TM = 512
A_SUB = 8
B_SUB = 4
FFN_TM = 1024
FFN_TF = 512
NB_ROWS = 4
NB_TOK = NB_ROWS * GRID_W
NB_WIN = 3 * NB_TOK
FFN_HALO = 16
FFN_CHUNKS = tuple((lo, min(lo + FFN_TF, D_FF)) for lo in range(0, D_FF, FFN_TF))


def _rms(x, g):
    return x * lax.rsqrt(jnp.mean(x * x, axis=-1, keepdims=True) + EPS) * g


def _proj_kernel(x_ref, g_ref, w_ref, c_ref, s1_ref, s2_ref,
                 qa_ref, kva_ref, qbl_ref, qbr_ref, kb_ref, vb_ref):
    hn = _rms(x_ref[...], g_ref[...]).astype(BF16)
    c = c_ref[...]
    s1 = s1_ref[...]
    s2 = s2_ref[...]
    tm = hn.shape[0]
    left = lax.broadcasted_iota(jnp.int32, (tm, LANES), 1) < HEAD_DIM
    scale = LOG2E / math.sqrt(HEAD_DIM)

    def rope(t):
        return t * c + pltpu.roll(t, 8, 1) * s1 + pltpu.roll(t, LANES - 8, 1) * s2

    def mm(lo, hi):
        return jnp.dot(hn, w_ref[:, lo:hi], preferred_element_type=F32)

    qa = mm(0, A_Q_W)
    for j in range(A_Q_W // LANES):
        sl = slice(j * LANES, (j + 1) * LANES)
        qa_ref[:, sl] = (rope(qa[:, sl]) * scale).astype(BF16)

    kv = mm(A_Q_W, A_Q_W + 2 * A_KV_W)
    ka = rope(kv[:, :LANES])
    va = kv[:, LANES:]
    zero = jnp.zeros_like(ka)
    for base, t in ((0, ka), (4, va)):
        sw = pltpu.roll(t, HEAD_DIM, 1)
        parts = (jnp.where(left, t, zero), jnp.where(left, zero, sw),
                 jnp.where(left, sw, zero), jnp.where(left, zero, t))
        for n, p in enumerate(parts):
            kva_ref[:, (base + n) * LANES:(base + n + 1) * LANES] = p.astype(BF16)

    o = A_Q_W + 2 * A_KV_W
    qb = mm(o, o + B_W) * scale
    for j in range(B_W // LANES):
        sl = slice(j * LANES, (j + 1) * LANES)
        t = qb[:, sl]
        qbl_ref[:, sl] = jnp.where(left, t, zero).astype(BF16)
        qbr_ref[:, sl] = jnp.where(left, zero, t).astype(BF16)
    kb_ref[...] = mm(o + B_W, o + 2 * B_W).astype(BF16)
    vb_ref[...] = mm(o + 2 * B_W, o + 3 * B_W).astype(BF16)


def _proj(x2, g, w, c, s1, s2, seq):
    t = x2.shape[0]
    tm = PROJ_TM
    nseq = seq // tm
    row = lambda i: (i, 0)
    const = lambda i: (0, 0)
    tab = lambda i: (i % nseq, 0)
    outs = [(A_Q_W, row), (8 * LANES, row), (B_W, row), (B_W, row), (B_W, row), (B_W, row)]
    return pl.pallas_call(
        _proj_kernel,
        grid=(t // tm,),
        in_specs=[pl.BlockSpec((tm, D_MODEL), row),
                  pl.BlockSpec((1, D_MODEL), const),
                  pl.BlockSpec(w.shape, const),
                  pl.BlockSpec((tm, LANES), tab),
                  pl.BlockSpec((tm, LANES), tab),
                  pl.BlockSpec((tm, LANES), tab)],
        out_specs=[pl.BlockSpec((tm, wd), im) for wd, im in outs],
        out_shape=[jax.ShapeDtypeStruct((t, wd), BF16) for wd, _ in outs],
        compiler_params=pltpu.CompilerParams(
            dimension_semantics=("arbitrary",), vmem_limit_bytes=VMEM_LIMIT),
        name="proj",
    )(x2, g, w, c, s1, s2)


def _attn_a_kernel(sink_ref, q_ref, kp_ref, kc_ref, kn_ref, g_ref, o_ref):
    i = pl.program_id(1)
    nstep = pl.num_programs(1)
    span = BLK + 2 * WINDOW
    a = lax.broadcasted_iota(jnp.int32, (BLK, span), 0)
    b = lax.broadcasted_iota(jnp.int32, (BLK, span), 1)
    d = b - a
    band = (d >= 0) & (d <= 2 * WINDOW)
    left = lax.broadcasted_iota(jnp.int32, (BLK, LANES), 1) < HEAD_DIM

    def key_block(idx, cols):
        if idx == 0:
            return kp_ref[:, cols]
        if idx <= A_SUB:
            return kc_ref[(idx - 1) * BLK:idx * BLK, cols]
        return kn_ref[:, cols]

    for j in range(A_SUB):
        valid = band
        if j == 0:
            valid = valid & (b >= jnp.where(i == 0, WINDOW, 0))
        if j == A_SUB - 1:
            valid = valid & (b < jnp.where(i == nstep - 1, WINDOW + BLK, span))
        rows = slice(j * BLK, (j + 1) * BLK)
        outs = []
        for g in range(A_KV_HEADS):
            def stack(col):
                return jnp.concatenate(
                    [key_block(j + w, slice((col + n) * LANES, (col + n + 1) * LANES))
                     for n in range(2) for w in range(3)], axis=0)
            kbd = stack(2 * g)
            vbd = stack(4 + 2 * g)
            q2 = jnp.concatenate(
                [q_ref[rows, (2 * g + p) * LANES:(2 * g + p + 1) * LANES] for p in range(2)], axis=0)
            s = lax.dot_general(q2, kbd, (((1,), (1,)), ((), ())), preferred_element_type=F32)
            prow = []
            dens = []
            for p in range(2):
                pcol = []
                dpair = []
                for hh in range(2):
                    sk = sink_ref[4 * g + 2 * p + hh]
                    sq = s[p * BLK:(p + 1) * BLK, hh * span:(hh + 1) * span]
                    sq = jnp.concatenate(
                        [jnp.where(valid[:, :WINDOW], sq[:, :WINDOW], NEG), sq[:, WINDOW:WINDOW + BLK],
                         jnp.where(valid[:, WINDOW + BLK:], sq[:, WINDOW + BLK:], NEG)], axis=1)
                    m = jnp.maximum(jnp.max(sq, axis=-1, keepdims=True), sk)
                    e = jnp.exp2(sq - m)
                    dpair.append(jnp.sum(e, axis=-1, keepdims=True) + jnp.exp2(sk - m))
                    pcol.append(e.astype(BF16))
                prow.append(jnp.concatenate(pcol, axis=1))
                dens.append(jnp.where(left, dpair[0], dpair[1]))
            pm = jnp.concatenate(prow, axis=0)
            o2 = jnp.dot(pm, vbd, preferred_element_type=F32)
            for p in range(2):
                outs.append(o2[p * BLK:(p + 1) * BLK] / dens[p])
        o = jnp.concatenate(outs, axis=1)
        o_ref[rows, :] = _rms(o, g_ref[...]).astype(BF16)


def _attn_a(sink, qa, kva, g, batch, seq):
    t = qa.shape[0]
    nblk = seq // BLK
    nstep = nblk // A_SUB
    cur = lambda b, i: (b * nstep + i, 0)
    prev = lambda b, i: (b * nblk + jnp.maximum(i * A_SUB - 1, 0), 0)
    nxt = lambda b, i: (b * nblk + jnp.minimum((i + 1) * A_SUB, nblk - 1), 0)
    const = lambda b, i: (0, 0)
    kw = kva.shape[1]
    return pl.pallas_call(
        _attn_a_kernel,
        grid=(batch, nstep),
        in_specs=[pl.BlockSpec(memory_space=pltpu.SMEM),
                  pl.BlockSpec((A_SUB * BLK, A_Q_W), cur),
                  pl.BlockSpec((BLK, kw), prev),
                  pl.BlockSpec((A_SUB * BLK, kw), cur),
                  pl.BlockSpec((BLK, kw), nxt),
                  pl.BlockSpec((1, A_Q_W), const)],
        out_specs=pl.BlockSpec((A_SUB * BLK, A_Q_W), cur),
        out_shape=jax.ShapeDtypeStruct((t, A_Q_W), BF16),
        compiler_params=pltpu.CompilerParams(
            dimension_semantics=("arbitrary", "arbitrary"), vmem_limit_bytes=VMEM_LIMIT),
        name="attn_a",
    )(sink, qa, kva, kva, kva, g)


def _attn_b_kernel(ql_ref, qr_ref, kp_ref, kc_ref, kn_ref, vp_ref, vc_ref, vn_ref,
                   bias_ref, g_ref, o_ref):
    js = pl.program_id(1)
    ngrp = pl.num_programs(1) * B_SUB
    left = lax.broadcasted_iota(jnp.int32, (NB_TOK, LANES), 1) < HEAD_DIM

    def win_block(refs, idx, cols):
        p_ref, c_ref, n_ref = refs
        if idx == 0:
            return p_ref[:, cols]
        if idx <= B_SUB:
            return c_ref[(idx - 1) * NB_TOK:idx * NB_TOK, cols]
        return n_ref[:, cols]

    for j in range(B_SUB):
        grp = js * B_SUB + j
        cfg = jnp.where(grp == 0, 0, jnp.where(grp == ngrp - 1, 2, 1))
        rows = slice(j * NB_TOK, (j + 1) * NB_TOK)
        outs = []
        for p in range(B_HEADS // 2):
            sl = slice(p * LANES, (p + 1) * LANES)
            q2 = jnp.concatenate([ql_ref[rows, sl], qr_ref[rows, sl]], axis=0)
            kw = jnp.concatenate([win_block((kp_ref, kc_ref, kn_ref), j + w, sl) for w in range(3)], axis=0)
            vw = jnp.concatenate([win_block((vp_ref, vc_ref, vn_ref), j + w, sl) for w in range(3)], axis=0)
            s = lax.dot_general(q2, kw, (((1,), (1,)), ((), ())), preferred_element_type=F32)
            ps = []
            ls = []
            for hh in range(2):
                sq = s[hh * NB_TOK:(hh + 1) * NB_TOK] + bias_ref[cfg, 2 * p + hh]
                m = jnp.max(sq, axis=-1, keepdims=True)
                e = jnp.exp2(sq - m)
                ls.append(jnp.sum(e, axis=-1, keepdims=True))
                ps.append(e.astype(BF16))
            o2 = jnp.dot(jnp.concatenate(ps, axis=0), vw, preferred_element_type=F32)
            outs.append(jnp.where(left, o2[:NB_TOK] / ls[0], o2[NB_TOK:] / ls[1]))
        o = jnp.concatenate(outs, axis=1)
        o_ref[rows, :] = _rms(o, g_ref[...]).astype(BF16)


def _attn_b(qbl, qbr, kb, vb, bias, g, batch, seq):
    t = qbl.shape[0]
    ngrp = seq // NB_TOK
    nstep = ngrp // B_SUB
    cur = lambda b, j: (b * nstep + j, 0)
    prev = lambda b, j: (b * ngrp + jnp.maximum(j * B_SUB - 1, 0), 0)
    nxt = lambda b, j: (b * ngrp + jnp.minimum((j + 1) * B_SUB, ngrp - 1), 0)
    const = lambda b, j: (0, 0)
    blk = lambda im: pl.BlockSpec((NB_TOK, B_W), im)
    big = pl.BlockSpec((B_SUB * NB_TOK, B_W), cur)
    return pl.pallas_call(
        _attn_b_kernel,
        grid=(batch, nstep),
        in_specs=[big, big, blk(prev), big, blk(nxt), blk(prev), big, blk(nxt),
                  pl.BlockSpec(bias.shape, lambda b, j: (0, 0, 0, 0), pipeline_mode=pl.Buffered(1)),
                  pl.BlockSpec((1, B_W), const)],
        out_specs=big,
        out_shape=jax.ShapeDtypeStruct((t, B_W), BF16),
        compiler_params=pltpu.CompilerParams(
            dimension_semantics=("arbitrary", "arbitrary"), vmem_limit_bytes=VMEM_LIMIT),
        name="attn_b",
    )(qbl, qbr, kb, kb, kb, vb, vb, vb, bias, g)


def _nb_row_ok(cfg, qr, wr):
    if cfg == 0:
        return wr >= NB_ROWS
    if cfg == 2:
        return wr < NA_KH
    return 0 <= wr - qr < NA_KH


def _nb_bias_kernel(u_ref, o_ref):
    lane = lax.broadcasted_iota(jnp.int32, (GRID_W, LANES), 1)
    cq = lax.broadcasted_iota(jnp.int32, (GRID_W, LANES), 0)
    kc = lane & (GRID_W - 1)
    cs = jnp.clip(cq - NA_KW // 2, 0, GRID_W - NA_KW)
    col_ok = (kc >= cs) & (kc < cs + NA_KW)
    lo_half = lane < GRID_W
    neg = jnp.full((GRID_W, LANES), NEG, F32)
    off = NA_KH - 1 - NB_ROWS
    for qr in range(NB_ROWS):
        for k in range(NB_WIN // LANES):
            a = 2 * k - qr + off
            w = pltpu.roll(jnp.broadcast_to(u_ref[0, a:a + 1, :], (GRID_W, LANES)), 0, 1,
                           stride=1, stride_axis=0)
            for cfg in range(3):
                ok_lo = _nb_row_ok(cfg, qr, 2 * k)
                ok_hi = _nb_row_ok(cfg, qr, 2 * k + 1)
                if ok_lo and ok_hi:
                    blk = jnp.where(col_ok, w, neg)
                elif ok_lo:
                    blk = jnp.where(col_ok & lo_half, w, neg)
                elif ok_hi:
                    blk = jnp.where(col_ok & jnp.logical_not(lo_half), w, neg)
                else:
                    blk = neg
                o_ref[cfg, 0, qr * GRID_W:(qr + 1) * GRID_W, k * LANES:(k + 1) * LANES] = blk


def _nb_bias(rpb_l):
    r = rpb_l.astype(F32) * LOG2E
    heads, na, nb = r.shape
    half = nb // 2
    gap = (LANES - 2 * nb) // 2
    z = jnp.zeros((heads, na - 1, gap), F32)
    u = jnp.concatenate([r[:, :-1, half:], z, r[:, 1:, :], z, r[:, :-1, :half]], axis=-1)
    return pl.pallas_call(
        _nb_bias_kernel,
        grid=(heads,),
        in_specs=[pl.BlockSpec((1, na - 1, LANES), lambda h: (h, 0, 0))],
        out_specs=pl.BlockSpec((3, 1, NB_TOK, NB_WIN), lambda h: (0, h, 0, 0)),
        out_shape=jax.ShapeDtypeStruct((3, heads, NB_TOK, NB_WIN), F32),
        compiler_params=pltpu.CompilerParams(
            dimension_semantics=("arbitrary",), vmem_limit_bytes=VMEM_LIMIT),
        name="nb_bias",
    )(u)


def _ffn_kernel(nseq, final, ap_ref, a_ref, an_ref, bp_ref, b_ref, bn_ref, xp_ref, x_ref, xx_ref,
                wo_ref, gf_ref, wg_ref, wu_ref, cw_ref, cb_ref, wd_ref, gl_ref,
                o_ref, xn_ref, h_ref, act_ref):
    i = pl.program_id(0)
    tm = x_ref.shape[0]
    hal = FFN_HALO
    rows = tm + 2 * hal
    mix = jnp.concatenate(
        [jnp.concatenate([ap_ref[...], a_ref[...], an_ref[...]], axis=0),
         jnp.concatenate([bp_ref[...], b_ref[...], bn_ref[...]], axis=0)], axis=1)
    xc = jnp.concatenate([xp_ref[...], x_ref[...], xx_ref[...]], axis=0)
    h = xc + jnp.dot(mix, wo_ref[...], preferred_element_type=F32)
    h_ref[...] = h[hal:hal + tm]
    first = (i % nseq) == 0
    last = (i % nseq) == nseq - 1
    r = lax.broadcasted_iota(jnp.int32, (rows, 1), 0)
    outside = (first & (r < hal)) | (last & (r >= hal + tm))
    xn_ref[...] = jnp.where(outside, 0.0, _rms(h, gf_ref[...])).astype(BF16)

    for lo, hi in FFN_CHUNKS:
        g = jnp.dot(xn_ref[...], wg_ref[:, lo:hi], preferred_element_type=F32)
        gc = (pltpu.roll(g, 1, 0)[hal:hal + tm] * cw_ref[0:1, lo:hi]
              + g[hal:hal + tm] * cw_ref[1:2, lo:hi]
              + pltpu.roll(g, rows - 1, 0)[hal:hal + tm] * cw_ref[2:3, lo:hi]
              + cb_ref[:, lo:hi])
        u = jnp.dot(xn_ref[hal:hal + tm, :], wu_ref[:, lo:hi], preferred_element_type=F32)
        act_ref[:, lo:hi] = (gc / (1.0 + jnp.exp(-gc)) * u).astype(BF16)

    out = h_ref[...] + jnp.dot(act_ref[...], wd_ref[...], preferred_element_type=F32)
    o_ref[...] = _rms(out, gl_ref[...]) if final else out


def _ffn(ma, mb, x2, wo, gf, wg, wu, cw, cb, wd, gl, seq, final):
    t = x2.shape[0]
    tm = FFN_TM
    nseq = seq // tm
    r = tm // FFN_HALO
    nhal = t // FFN_HALO
    row = lambda i: (i, 0)
    prev = lambda i: (jnp.maximum(i * r - 1, 0), 0)
    nxt = lambda i: (jnp.minimum((i + 1) * r, nhal - 1), 0)
    const = lambda i: (0, 0)
    once = lambda shape: pl.BlockSpec(shape, const, pipeline_mode=pl.Buffered(1))
    halo3 = lambda w: [pl.BlockSpec((FFN_HALO, w), prev), pl.BlockSpec((tm, w), row),
                       pl.BlockSpec((FFN_HALO, w), nxt)]
    return pl.pallas_call(
        functools.partial(_ffn_kernel, nseq, final),
        grid=(t // tm,),
        in_specs=halo3(A_Q_W) + halo3(B_W) + halo3(D_MODEL) + [
            once(wo.shape), once((1, D_MODEL)), once(wg.shape), once(wu.shape), once(cw.shape),
            once(cb.shape), once(wd.shape), once((1, D_MODEL))],
        out_specs=pl.BlockSpec((tm, D_MODEL), row),
        out_shape=jax.ShapeDtypeStruct((t, D_MODEL), F32),
        scratch_shapes=[pltpu.VMEM((tm + 2 * FFN_HALO, D_MODEL), BF16),
                        pltpu.VMEM((tm, D_MODEL), F32),
                        pltpu.VMEM((tm, D_FF), BF16)],
        compiler_params=pltpu.CompilerParams(
            dimension_semantics=("arbitrary",), vmem_limit_bytes=VMEM_LIMIT),
        name="ffn",
    )(ma, ma, ma, mb, mb, mb, x2, x2, x2, wo, gf, wg, wu, cw, cb, wd, gl)


def _rope_tables(seq):
    f32 = np.float32
    pos = np.arange(seq, dtype=np.float64)
    inv_freq = ROPE_THETA ** (-np.arange(0, ROT_DIM, 2, dtype=np.float64) / ROT_DIM)
    ang = pos[:, None] * inv_freq[None, :]
    cos = np.cos(ang).astype(f32)
    sin = np.sin(ang).astype(f32)
    half = ROT_DIM // 2
    ones = np.ones((seq, HEAD_DIM - ROT_DIM), f32)
    zeros = np.zeros((seq, HEAD_DIM - ROT_DIM), f32)
    zh = np.zeros((seq, half), f32)
    c = np.concatenate([cos, cos, ones], axis=1)
    s1 = np.concatenate([zh, sin, zeros], axis=1)
    s2 = np.concatenate([-sin, zh, zeros], axis=1)
    rep = LANES // HEAD_DIM
    return tuple(jnp.asarray(np.tile(t, (1, rep))) for t in (c, s1, s2))


def kernel(x, norm_mix, w_in, sink, rpb, norm_a, norm_b, w_out, norm_ffn,
           w_gate, w_up, conv_w, conv_b, w_down, norm_final):
    batch, seq, d = x.shape
    depth = w_in.shape[0]
    assert d == D_MODEL and seq % PROJ_TM == 0 and seq % FFN_TM == 0 and seq % (A_SUB * BLK) == 0
    assert seq % (B_SUB * NB_TOK) == 0 and seq // GRID_W >= 3 * NB_ROWS
    c, s1, s2 = _rope_tables(seq)
    h = x.reshape(batch * seq, d)
    for l in range(depth):
        qa, kva, qbl, qbr, kb, vb = _proj(h, norm_mix[l][None], w_in[l].astype(BF16), c, s1, s2, seq)
        ma = _attn_a(sink[l].astype(F32) * LOG2E, qa, kva, norm_a[l][None].astype(F32), batch, seq)
        mb = _attn_b(qbl, qbr, kb, vb, _nb_bias(rpb[l]), norm_b[l][None].astype(F32), batch, seq)
        h = _ffn(ma, mb, h, w_out[l].astype(BF16), norm_ffn[l][None], w_gate[l].astype(BF16), w_up[l].astype(BF16),
                 conv_w[l], conv_b[l][None], w_down[l].astype(BF16), norm_final[None], seq,
                 final=(l == depth - 1))
    return h.reshape(batch, seq, d)
```

```python
import functools
import math

import numpy as np
import jax
import jax.numpy as jnp
from jax import lax
from jax.experimental import pallas as pl
from jax.experimental.pallas import tpu as pltpu

D_MODEL = 1024
HEAD_DIM = 64
A_HEADS = 8
A_KV_HEADS = 2
B_HEADS = 8
A_Q_W = A_HEADS * HEAD_DIM
A_KV_W = A_KV_HEADS * HEAD_DIM
B_W = B_HEADS * HEAD_DIM
WINDOW = 128
BLK = 128
ROPE_THETA = 500000.0
ROT_DIM = HEAD_DIM // 4
GRID_W = 64
NA_KH = 8
NA_KW = 16
D_FF = 2816
EPS = 1e-6
NEG = -1e30
LOG2E = math.log2(math.e)

LANES = 128
BF16_ROWS = 16
VMEM_LIMIT = 56 * 1024 * 1024

F32 = jnp.float32
BF16 = jnp.bfloat16

PROJ_TM = 1024
A_SUB = 8
B_SUB = 4
FFN_TM = 1024
FFN_TF = 512
NB_ROWS = 4
NB_TOK = NB_ROWS * GRID_W
NB_WIN = 3 * NB_TOK
FFN_HALO = 16
FFN_CHUNKS = tuple((lo, min(lo + FFN_TF, D_FF)) for lo in range(0, D_FF, FFN_TF))


def _rms(x, g):
    return x * lax.rsqrt(jnp.mean(x * x, axis=-1, keepdims=True) + EPS) * g


def _proj_kernel(x_ref, g_ref, w_ref, c_ref, s1_ref, s2_ref, *refs):
    ncast = (len(refs) - 6) // 2
    cast_in = refs[:ncast]
    qa_ref, kva_ref, qbl_ref, qbr_ref, kb_ref, vb_ref = refs[ncast:ncast + 6]
    cast_out = refs[ncast + 6:]
    hn = _rms(x_ref[...], g_ref[...]).astype(BF16)
    c = c_ref[...]
    s1 = s1_ref[...]
    s2 = s2_ref[...]
    tm = hn.shape[0]
    left = lax.broadcasted_iota(jnp.int32, (tm, LANES), 1) < HEAD_DIM
    scale = LOG2E / math.sqrt(HEAD_DIM)

    def rope(t):
        return t * c + pltpu.roll(t, 8, 1) * s1 + pltpu.roll(t, LANES - 8, 1) * s2

    def mm(lo, hi):
        return jnp.dot(hn, w_ref[:, lo:hi], preferred_element_type=F32)

    qa = mm(0, A_Q_W)
    for j in range(A_Q_W // LANES):
        sl = slice(j * LANES, (j + 1) * LANES)
        qa_ref[:, sl] = (rope(qa[:, sl]) * scale).astype(BF16)

    kv = mm(A_Q_W, A_Q_W + 2 * A_KV_W)
    ka = rope(kv[:, :LANES])
    va = kv[:, LANES:]
    zero = jnp.zeros_like(ka)
    for base, t in ((0, ka), (4, va)):
        sw = pltpu.roll(t, HEAD_DIM, 1)
        parts = (jnp.where(left, t, zero), jnp.where(left, zero, sw),
                 jnp.where(left, sw, zero), jnp.where(left, zero, t))
        for n, p in enumerate(parts):
            kva_ref[:, (base + n) * LANES:(base + n + 1) * LANES] = p.astype(BF16)

    o = A_Q_W + 2 * A_KV_W
    qb = mm(o, o + B_W) * scale
    for j in range(B_W // LANES):
        sl = slice(j * LANES, (j + 1) * LANES)
        t = qb[:, sl]
        qbl_ref[:, sl] = jnp.where(left, t, zero).astype(BF16)
        qbr_ref[:, sl] = jnp.where(left, zero, t).astype(BF16)
    kb_ref[...] = mm(o + B_W, o + 2 * B_W).astype(BF16)
    vb_ref[...] = mm(o + 2 * B_W, o + 3 * B_W).astype(BF16)
    for src, dst in zip(cast_in, cast_out):
        dst[...] = src[...].astype(BF16)


def _slab_spec(shape, nsteps):
    rows, cols = shape
    nslab = max(n for n in range(1, nsteps + 1)
                if nsteps % n == 0 and rows % n == 0 and (rows // n) % BF16_ROWS == 0)
    rep = nsteps // nslab
    return pl.BlockSpec((rows // nslab, cols), lambda i: (i // rep, 0))


def _proj(x2, g, w, c, s1, s2, seq, to_cast):
    t = x2.shape[0]
    tm = PROJ_TM
    nseq = seq // tm
    nsteps = t // tm
    row = lambda i: (i, 0)
    const = lambda i: (0, 0)
    tab = lambda i: (i % nseq, 0)
    outs = [(A_Q_W, row), (8 * LANES, row), (B_W, row), (B_W, row), (B_W, row), (B_W, row)]
    slabs = [_slab_spec(m.shape, nsteps) for m in to_cast]
    return pl.pallas_call(
        _proj_kernel,
        grid=(nsteps,),
        in_specs=[pl.BlockSpec((tm, D_MODEL), row),
                  pl.BlockSpec((1, D_MODEL), const),
                  pl.BlockSpec(w.shape, const),
                  pl.BlockSpec((tm, LANES), tab),
                  pl.BlockSpec((tm, LANES), tab),
                  pl.BlockSpec((tm, LANES), tab)] + slabs,
        out_specs=[pl.BlockSpec((tm, wd), im) for wd, im in outs] + slabs,
        out_shape=[jax.ShapeDtypeStruct((t, wd), BF16) for wd, _ in outs]
        + [jax.ShapeDtypeStruct(m.shape, BF16) for m in to_cast],
        compiler_params=pltpu.CompilerParams(
            dimension_semantics=("arbitrary",), vmem_limit_bytes=VMEM_LIMIT),
        name="proj",
    )(x2, g, w, c, s1, s2, *to_cast)


def _attn_a_kernel(sink_ref, q_ref, kp_ref, kc_ref, kn_ref, g_ref, o_ref):
    i = pl.program_id(1)
    nstep = pl.num_programs(1)
    span = BLK + 2 * WINDOW
    a = lax.broadcasted_iota(jnp.int32, (BLK, span), 0)
    b = lax.broadcasted_iota(jnp.int32, (BLK, span), 1)
    d = b - a
    band = (d >= 0) & (d <= 2 * WINDOW)
    left = lax.broadcasted_iota(jnp.int32, (BLK, LANES), 1) < HEAD_DIM

    def key_block(idx, cols):
        if idx == 0:
            return kp_ref[:, cols]
        if idx <= A_SUB:
            return kc_ref[(idx - 1) * BLK:idx * BLK, cols]
        return kn_ref[:, cols]

    for j in range(A_SUB):
        valid = band
        if j == 0:
            valid = valid & (b >= jnp.where(i == 0, WINDOW, 0))
        if j == A_SUB - 1:
            valid = valid & (b < jnp.where(i == nstep - 1, WINDOW + BLK, span))
        rows = slice(j * BLK, (j + 1) * BLK)
        outs = []
        for g in range(A_KV_HEADS):
            def stack(col):
                return jnp.concatenate(
                    [key_block(j + w, slice((col + n) * LANES, (col + n + 1) * LANES))
                     for n in range(2) for w in range(3)], axis=0)
            kbd = stack(2 * g)
            vbd = stack(4 + 2 * g)
            q2 = jnp.concatenate(
                [q_ref[rows, (2 * g + p) * LANES:(2 * g + p + 1) * LANES] for p in range(2)], axis=0)
            s = lax.dot_general(q2, kbd, (((1,), (1,)), ((), ())), preferred_element_type=F32)
            prow = []
            dens = []
            for p in range(2):
                pcol = []
                dpair = []
                for hh in range(2):
                    sk = sink_ref[4 * g + 2 * p + hh]
                    sq = s[p * BLK:(p + 1) * BLK, hh * span:(hh + 1) * span]
                    sq = jnp.concatenate(
                        [jnp.where(valid[:, :WINDOW], sq[:, :WINDOW], NEG), sq[:, WINDOW:WINDOW + BLK],
                         jnp.where(valid[:, WINDOW + BLK:], sq[:, WINDOW + BLK:], NEG)], axis=1)
                    m = jnp.maximum(jnp.max(sq, axis=-1, keepdims=True), sk)
                    e = jnp.exp2(sq - m)
                    dpair.append(jnp.sum(e, axis=-1, keepdims=True) + jnp.exp2(sk - m))
                    pcol.append(e.astype(BF16))
                prow.append(jnp.concatenate(pcol, axis=1))
                dens.append(jnp.where(left, dpair[0], dpair[1]))
            pm = jnp.concatenate(prow, axis=0)
            o2 = jnp.dot(pm, vbd, preferred_element_type=F32)
            for p in range(2):
                outs.append(o2[p * BLK:(p + 1) * BLK] / dens[p])
        o = jnp.concatenate(outs, axis=1)
        o_ref[rows, :] = _rms(o, g_ref[...]).astype(BF16)


def _attn_a(sink, qa, kva, g, batch, seq):
    t = qa.shape[0]
    nblk = seq // BLK
    nstep = nblk // A_SUB
    cur = lambda b, i: (b * nstep + i, 0)
    prev = lambda b, i: (b * nblk + jnp.maximum(i * A_SUB - 1, 0), 0)
    nxt = lambda b, i: (b * nblk + jnp.minimum((i + 1) * A_SUB, nblk - 1), 0)
    const = lambda b, i: (0, 0)
    kw = kva.shape[1]
    return pl.pallas_call(
        _attn_a_kernel,
        grid=(batch, nstep),
        in_specs=[pl.BlockSpec(memory_space=pltpu.SMEM),
                  pl.BlockSpec((A_SUB * BLK, A_Q_W), cur),
                  pl.BlockSpec((BLK, kw), prev),
                  pl.BlockSpec((A_SUB * BLK, kw), cur),
                  pl.BlockSpec((BLK, kw), nxt),
                  pl.BlockSpec((1, A_Q_W), const)],
        out_specs=pl.BlockSpec((A_SUB * BLK, A_Q_W), cur),
        out_shape=jax.ShapeDtypeStruct((t, A_Q_W), BF16),
        compiler_params=pltpu.CompilerParams(
            dimension_semantics=("arbitrary", "arbitrary"), vmem_limit_bytes=VMEM_LIMIT),
        name="attn_a",
    )(sink, qa, kva, kva, kva, g)


def _attn_b_kernel(ql_ref, qr_ref, kp_ref, kc_ref, kn_ref, vp_ref, vc_ref, vn_ref,
                   bias_ref, g_ref, o_ref):
    js = pl.program_id(1)
    ngrp = pl.num_programs(1) * B_SUB
    left = lax.broadcasted_iota(jnp.int32, (NB_TOK, LANES), 1) < HEAD_DIM

    def win_block(refs, idx, cols):
        p_ref, c_ref, n_ref = refs
        if idx == 0:
            return p_ref[:, cols]
        if idx <= B_SUB:
            return c_ref[(idx - 1) * NB_TOK:idx * NB_TOK, cols]
        return n_ref[:, cols]

    for j in range(B_SUB):
        grp = js * B_SUB + j
        cfg = jnp.where(grp == 0, 0, jnp.where(grp == ngrp - 1, 2, 1))
        rows = slice(j * NB_TOK, (j + 1) * NB_TOK)
        outs = []
        for p in range(B_HEADS // 2):
            sl = slice(p * LANES, (p + 1) * LANES)
            q2 = jnp.concatenate([ql_ref[rows, sl], qr_ref[rows, sl]], axis=0)
            kw = jnp.concatenate([win_block((kp_ref, kc_ref, kn_ref), j + w, sl) for w in range(3)], axis=0)
            vw = jnp.concatenate([win_block((vp_ref, vc_ref, vn_ref), j + w, sl) for w in range(3)], axis=0)
            s = lax.dot_general(q2, kw, (((1,), (1,)), ((), ())), preferred_element_type=F32)
            ps = []
            ls = []
            for hh in range(2):
                sq = s[hh * NB_TOK:(hh + 1) * NB_TOK] + bias_ref[cfg, 2 * p + hh]
                m = jnp.max(sq, axis=-1, keepdims=True)
                e = jnp.exp2(sq - m)
                ls.append(jnp.sum(e, axis=-1, keepdims=True))
                ps.append(e.astype(BF16))
            o2 = jnp.dot(jnp.concatenate(ps, axis=0), vw, preferred_element_type=F32)
            outs.append(jnp.where(left, o2[:NB_TOK] / ls[0], o2[NB_TOK:] / ls[1]))
        o = jnp.concatenate(outs, axis=1)
        o_ref[rows, :] = _rms(o, g_ref[...]).astype(BF16)


def _attn_b(qbl, qbr, kb, vb, bias, g, batch, seq):
    t = qbl.shape[0]
    ngrp = seq // NB_TOK
    nstep = ngrp // B_SUB
    cur = lambda b, j: (b * nstep + j, 0)
    prev = lambda b, j: (b * ngrp + jnp.maximum(j * B_SUB - 1, 0), 0)
    nxt = lambda b, j: (b * ngrp + jnp.minimum((j + 1) * B_SUB, ngrp - 1), 0)
    const = lambda b, j: (0, 0)
    blk = lambda im: pl.BlockSpec((NB_TOK, B_W), im)
    big = pl.BlockSpec((B_SUB * NB_TOK, B_W), cur)
    return pl.pallas_call(
        _attn_b_kernel,
        grid=(batch, nstep),
        in_specs=[big, big, blk(prev), big, blk(nxt), blk(prev), big, blk(nxt),
                  pl.BlockSpec(bias.shape, lambda b, j: (0, 0, 0, 0), pipeline_mode=pl.Buffered(1)),
                  pl.BlockSpec((1, B_W), const)],
        out_specs=big,
        out_shape=jax.ShapeDtypeStruct((t, B_W), BF16),
        compiler_params=pltpu.CompilerParams(
            dimension_semantics=("arbitrary", "arbitrary"), vmem_limit_bytes=VMEM_LIMIT),
        name="attn_b",
    )(qbl, qbr, kb, kb, kb, vb, vb, vb, bias, g)


def _nb_row_ok(cfg, qr, wr):
    if cfg == 0:
        return wr >= NB_ROWS
    if cfg == 2:
        return wr < NA_KH
    return 0 <= wr - qr < NA_KH


def _nb_bias_kernel(u_ref, o_ref):
    lane = lax.broadcasted_iota(jnp.int32, (GRID_W, LANES), 1)
    cq = lax.broadcasted_iota(jnp.int32, (GRID_W, LANES), 0)
    kc = lane & (GRID_W - 1)
    cs = jnp.clip(cq - NA_KW // 2, 0, GRID_W - NA_KW)
    col_ok = (kc >= cs) & (kc < cs + NA_KW)
    lo_half = lane < GRID_W
    neg = jnp.full((GRID_W, LANES), NEG, F32)
    off = NA_KH - 1 - NB_ROWS
    for qr in range(NB_ROWS):
        for k in range(NB_WIN // LANES):
            a = 2 * k - qr + off
            w = pltpu.roll(jnp.broadcast_to(u_ref[0, a:a + 1, :], (GRID_W, LANES)), 0, 1,
                           stride=1, stride_axis=0)
            for cfg in range(3):
                ok_lo = _nb_row_ok(cfg, qr, 2 * k)
                ok_hi = _nb_row_ok(cfg, qr, 2 * k + 1)
                if ok_lo and ok_hi:
                    blk = jnp.where(col_ok, w, neg)
                elif ok_lo:
                    blk = jnp.where(col_ok & lo_half, w, neg)
                elif ok_hi:
                    blk = jnp.where(col_ok & jnp.logical_not(lo_half), w, neg)
                else:
                    blk = neg
                o_ref[cfg, 0, qr * GRID_W:(qr + 1) * GRID_W, k * LANES:(k + 1) * LANES] = blk


def _nb_bias(rpb_l):
    r = rpb_l.astype(F32) * LOG2E
    heads, na, nb = r.shape
    half = nb // 2
    gap = (LANES - 2 * nb) // 2
    z = jnp.zeros((heads, na - 1, gap), F32)
    u = jnp.concatenate([r[:, :-1, half:], z, r[:, 1:, :], z, r[:, :-1, :half]], axis=-1)
    return pl.pallas_call(
        _nb_bias_kernel,
        grid=(heads,),
        in_specs=[pl.BlockSpec((1, na - 1, LANES), lambda h: (h, 0, 0))],
        out_specs=pl.BlockSpec((3, 1, NB_TOK, NB_WIN), lambda h: (0, h, 0, 0)),
        out_shape=jax.ShapeDtypeStruct((3, heads, NB_TOK, NB_WIN), F32),
        compiler_params=pltpu.CompilerParams(
            dimension_semantics=("arbitrary",), vmem_limit_bytes=VMEM_LIMIT),
        name="nb_bias",
    )(u)


def _ffn_kernel(nseq, final, ap_ref, a_ref, an_ref, bp_ref, b_ref, bn_ref, xp_ref, x_ref, xx_ref,
                wo_ref, gf_ref, wg_ref, wu_ref, cw_ref, cb_ref, wd_ref, gl_ref,
                o_ref, xn_ref, h_ref, act_ref):
    i = pl.program_id(0)
    tm = x_ref.shape[0]
    hal = FFN_HALO
    rows = tm + 2 * hal
    mix = jnp.concatenate(
        [jnp.concatenate([ap_ref[...], a_ref[...], an_ref[...]], axis=0),
         jnp.concatenate([bp_ref[...], b_ref[...], bn_ref[...]], axis=0)], axis=1)
    xc = jnp.concatenate([xp_ref[...], x_ref[...], xx_ref[...]], axis=0)
    h = xc + jnp.dot(mix, wo_ref[...], preferred_element_type=F32)
    h_ref[...] = h[hal:hal + tm]
    first = (i % nseq) == 0
    last = (i % nseq) == nseq - 1
    r = lax.broadcasted_iota(jnp.int32, (rows, 1), 0)
    outside = (first & (r < hal)) | (last & (r >= hal + tm))
    xn_ref[...] = jnp.where(outside, 0.0, _rms(h, gf_ref[...])).astype(BF16)

    for lo, hi in FFN_CHUNKS:
        g = jnp.dot(xn_ref[...], wg_ref[:, lo:hi], preferred_element_type=F32)
        gc = (pltpu.roll(g, 1, 0)[hal:hal + tm] * cw_ref[0:1, lo:hi]
              + g[hal:hal + tm] * cw_ref[1:2, lo:hi]
              + pltpu.roll(g, rows - 1, 0)[hal:hal + tm] * cw_ref[2:3, lo:hi]
              + cb_ref[:, lo:hi])
        u = jnp.dot(xn_ref[hal:hal + tm, :], wu_ref[:, lo:hi], preferred_element_type=F32)
        act_ref[:, lo:hi] = (gc / (1.0 + jnp.exp(-gc)) * u).astype(BF16)

    out = h_ref[...] + jnp.dot(act_ref[...], wd_ref[...], preferred_element_type=F32)
    o_ref[...] = _rms(out, gl_ref[...]) if final else out


def _ffn(ma, mb, x2, wo, gf, wg, wu, cw, cb, wd, gl, seq, final):
    t = x2.shape[0]
    tm = FFN_TM
    nseq = seq // tm
    r = tm // FFN_HALO
    nhal = t // FFN_HALO
    row = lambda i: (i, 0)
    prev = lambda i: (jnp.maximum(i * r - 1, 0), 0)
    nxt = lambda i: (jnp.minimum((i + 1) * r, nhal - 1), 0)
    const = lambda i: (0, 0)
    once = lambda shape: pl.BlockSpec(shape, const, pipeline_mode=pl.Buffered(1))
    halo3 = lambda w: [pl.BlockSpec((FFN_HALO, w), prev), pl.BlockSpec((tm, w), row),
                       pl.BlockSpec((FFN_HALO, w), nxt)]
    return pl.pallas_call(
        functools.partial(_ffn_kernel, nseq, final),
        grid=(t // tm,),
        in_specs=halo3(A_Q_W) + halo3(B_W) + halo3(D_MODEL) + [
            once(wo.shape), once((1, D_MODEL)), once(wg.shape), once(wu.shape), once(cw.shape),
            once(cb.shape), once(wd.shape), once((1, D_MODEL))],
        out_specs=pl.BlockSpec((tm, D_MODEL), row),
        out_shape=jax.ShapeDtypeStruct((t, D_MODEL), F32),
        scratch_shapes=[pltpu.VMEM((tm + 2 * FFN_HALO, D_MODEL), BF16),
                        pltpu.VMEM((tm, D_MODEL), F32),
                        pltpu.VMEM((tm, D_FF), BF16)],
        compiler_params=pltpu.CompilerParams(
            dimension_semantics=("arbitrary",), vmem_limit_bytes=VMEM_LIMIT),
        name="ffn",
    )(ma, ma, ma, mb, mb, mb, x2, x2, x2, wo, gf, wg, wu, cw, cb, wd, gl)


def _rope_tables(seq):
    f32 = np.float32
    pos = np.arange(seq, dtype=np.float64)
    inv_freq = ROPE_THETA ** (-np.arange(0, ROT_DIM, 2, dtype=np.float64) / ROT_DIM)
    ang = pos[:, None] * inv_freq[None, :]
    cos = np.cos(ang).astype(f32)
    sin = np.sin(ang).astype(f32)
    half = ROT_DIM // 2
    ones = np.ones((seq, HEAD_DIM - ROT_DIM), f32)
    zeros = np.zeros((seq, HEAD_DIM - ROT_DIM), f32)
    zh = np.zeros((seq, half), f32)
    c = np.concatenate([cos, cos, ones], axis=1)
    s1 = np.concatenate([zh, sin, zeros], axis=1)
    s2 = np.concatenate([-sin, zh, zeros], axis=1)
    rep = LANES // HEAD_DIM
    return tuple(jnp.asarray(np.tile(t, (1, rep))) for t in (c, s1, s2))


def kernel(x, norm_mix, w_in, sink, rpb, norm_a, norm_b, w_out, norm_ffn,
           w_gate, w_up, conv_w, conv_b, w_down, norm_final):
    batch, seq, d = x.shape
    depth = w_in.shape[0]
    assert d == D_MODEL and seq % PROJ_TM == 0 and seq % FFN_TM == 0 and seq % (A_SUB * BLK) == 0
    assert seq % (B_SUB * NB_TOK) == 0 and seq // GRID_W >= 3 * NB_ROWS
    c, s1, s2 = _rope_tables(seq)
    h = x.reshape(batch * seq, d)
    for l in range(depth):
        qa, kva, qbl, qbr, kb, vb, wo, wg, wu, wd = _proj(
            h, norm_mix[l][None], w_in[l].astype(BF16), c, s1, s2, seq,
            [m.astype(F32) for m in (w_out[l], w_gate[l], w_up[l], w_down[l])])
        ma = _attn_a(sink[l].astype(F32) * LOG2E, qa, kva, norm_a[l][None].astype(F32), batch, seq)
        mb = _attn_b(qbl, qbr, kb, vb, _nb_bias(rpb[l]), norm_b[l][None].astype(F32), batch, seq)
        h = _ffn(ma, mb, h, wo, norm_ffn[l][None], wg, wu, conv_w[l], conv_b[l][None], wd,
                 norm_final[None], seq, final=(l == depth - 1))
    return h.reshape(batch, seq, d)
```

```python
import functools
import math

import numpy as np
import jax
import jax.numpy as jnp
from jax import lax
from jax.experimental import pallas as pl
from jax.experimental.pallas import tpu as pltpu

D_MODEL = 1024
HEAD_DIM = 64
A_HEADS = 8
A_KV_HEADS = 2
B_HEADS = 8
A_Q_W = A_HEADS * HEAD_DIM
A_KV_W = A_KV_HEADS * HEAD_DIM
B_W = B_HEADS * HEAD_DIM
WINDOW = 128
BLK = 128
ROPE_THETA = 500000.0
ROT_DIM = HEAD_DIM // 4
GRID_W = 64
NA_KH = 8
NA_KW = 16
D_FF = 2816
EPS = 1e-6
NEG = -1e30
LOG2E = math.log2(math.e)

LANES = 128
BF16_ROWS = 16
VMEM_LIMIT = 56 * 1024 * 1024

F32 = jnp.float32
BF16 = jnp.bfloat16

PROJ_TM = 1024
A_SUB = 8
B_SUB = 4
FFN_TM = 1024
FFN_TF = 1024
NB_ROWS = 4
NB_TOK = NB_ROWS * GRID_W
NB_WIN = 3 * NB_TOK
FFN_HALO = 16
FFN_CHUNKS = tuple((lo, min(lo + FFN_TF, D_FF)) for lo in range(0, D_FF, FFN_TF))


def _rms(x, g):
    return x * lax.rsqrt(jnp.mean(x * x, axis=-1, keepdims=True) + EPS) * g


def _proj_kernel(x_ref, g_ref, w_ref, c_ref, s1_ref, s2_ref, *refs):
    ncast = (len(refs) - 6) // 2
    cast_in = refs[:ncast]
    qa_ref, kva_ref, qbl_ref, qbr_ref, kb_ref, vb_ref = refs[ncast:ncast + 6]
    cast_out = refs[ncast + 6:]
    hn = _rms(x_ref[...], g_ref[...]).astype(BF16)
    c = c_ref[...]
    s1 = s1_ref[...]
    s2 = s2_ref[...]
    tm = hn.shape[0]
    left = lax.broadcasted_iota(jnp.int32, (tm, LANES), 1) < HEAD_DIM
    scale = LOG2E / math.sqrt(HEAD_DIM)

    def rope(t):
        return t * c + pltpu.roll(t, 8, 1) * s1 + pltpu.roll(t, LANES - 8, 1) * s2

    def mm(lo, hi):
        return jnp.dot(hn, w_ref[:, lo:hi], preferred_element_type=F32)

    qa = mm(0, A_Q_W)
    for j in range(A_Q_W // LANES):
        sl = slice(j * LANES, (j + 1) * LANES)
        qa_ref[:, sl] = (rope(qa[:, sl]) * scale).astype(BF16)

    kv = mm(A_Q_W, A_Q_W + 2 * A_KV_W)
    ka = rope(kv[:, :LANES])
    va = kv[:, LANES:]
    zero = jnp.zeros_like(ka)
    for base, t in ((0, ka), (4, va)):
        sw = pltpu.roll(t, HEAD_DIM, 1)
        parts = (jnp.where(left, t, zero), jnp.where(left, zero, sw),
                 jnp.where(left, sw, zero), jnp.where(left, zero, t))
        for n, p in enumerate(parts):
            kva_ref[:, (base + n) * LANES:(base + n + 1) * LANES] = p.astype(BF16)

    o = A_Q_W + 2 * A_KV_W
    qb = mm(o, o + B_W) * scale
    for j in range(B_W // LANES):
        sl = slice(j * LANES, (j + 1) * LANES)
        t = qb[:, sl]
        qbl_ref[:, sl] = jnp.where(left, t, zero).astype(BF16)
        qbr_ref[:, sl] = jnp.where(left, zero, t).astype(BF16)
    kb_ref[...] = mm(o + B_W, o + 2 * B_W).astype(BF16)
    vb_ref[...] = mm(o + 2 * B_W, o + 3 * B_W).astype(BF16)
    for src, dst in zip(cast_in, cast_out):
        dst[...] = src[...].astype(BF16)


def _slab_spec(shape, nsteps):
    rows, cols = shape
    nslab = max(n for n in range(1, nsteps + 1)
                if nsteps % n == 0 and rows % n == 0 and (rows // n) % BF16_ROWS == 0)
    rep = nsteps // nslab
    return pl.BlockSpec((rows // nslab, cols), lambda i: (i // rep, 0))


def _proj(x2, g, w, c, s1, s2, seq, to_cast):
    t = x2.shape[0]
    tm = PROJ_TM
    nseq = seq // tm
    nsteps = t // tm
    row = lambda i: (i, 0)
    const = lambda i: (0, 0)
    tab = lambda i: (i % nseq, 0)
    outs = [(A_Q_W, row), (8 * LANES, row), (B_W, row), (B_W, row), (B_W, row), (B_W, row)]
    slabs = [_slab_spec(m.shape, nsteps) for m in to_cast]
    return pl.pallas_call(
        _proj_kernel,
        grid=(nsteps,),
        in_specs=[pl.BlockSpec((tm, D_MODEL), row),
                  pl.BlockSpec((1, D_MODEL), const),
                  pl.BlockSpec(w.shape, const),
                  pl.BlockSpec((tm, LANES), tab),
                  pl.BlockSpec((tm, LANES), tab),
                  pl.BlockSpec((tm, LANES), tab)] + slabs,
        out_specs=[pl.BlockSpec((tm, wd), im) for wd, im in outs] + slabs,
        out_shape=[jax.ShapeDtypeStruct((t, wd), BF16) for wd, _ in outs]
        + [jax.ShapeDtypeStruct(m.shape, BF16) for m in to_cast],
        compiler_params=pltpu.CompilerParams(
            dimension_semantics=("arbitrary",), vmem_limit_bytes=VMEM_LIMIT),
        name="proj",
    )(x2, g, w, c, s1, s2, *to_cast)


def _attn_a_kernel(sink_ref, q_ref, kp_ref, kc_ref, kn_ref, g_ref, o_ref):
    i = pl.program_id(1)
    nstep = pl.num_programs(1)
    span = BLK + 2 * WINDOW
    a = lax.broadcasted_iota(jnp.int32, (BLK, span), 0)
    b = lax.broadcasted_iota(jnp.int32, (BLK, span), 1)
    d = b - a
    band = (d >= 0) & (d <= 2 * WINDOW)
    left = lax.broadcasted_iota(jnp.int32, (BLK, LANES), 1) < HEAD_DIM

    def key_block(idx, cols):
        if idx == 0:
            return kp_ref[:, cols]
        if idx <= A_SUB:
            return kc_ref[(idx - 1) * BLK:idx * BLK, cols]
        return kn_ref[:, cols]

    for j in range(A_SUB):
        valid = band
        if j == 0:
            valid = valid & (b >= jnp.where(i == 0, WINDOW, 0))
        if j == A_SUB - 1:
            valid = valid & (b < jnp.where(i == nstep - 1, WINDOW + BLK, span))
        rows = slice(j * BLK, (j + 1) * BLK)
        outs = []
        for g in range(A_KV_HEADS):
            def stack(col):
                return jnp.concatenate(
                    [key_block(j + w, slice((col + n) * LANES, (col + n + 1) * LANES))
                     for n in range(2) for w in range(3)], axis=0)
            kbd = stack(2 * g)
            vbd = stack(4 + 2 * g)
            q2 = jnp.concatenate(
                [q_ref[rows, (2 * g + p) * LANES:(2 * g + p + 1) * LANES] for p in range(2)], axis=0)
            s = lax.dot_general(q2, kbd, (((1,), (1,)), ((), ())), preferred_element_type=F32)
            prow = []
            dens = []
            for p in range(2):
                pcol = []
                dpair = []
                for hh in range(2):
                    sk = sink_ref[4 * g + 2 * p + hh]
                    sq = s[p * BLK:(p + 1) * BLK, hh * span:(hh + 1) * span]
                    sq = jnp.concatenate(
                        [jnp.where(valid[:, :WINDOW], sq[:, :WINDOW], NEG), sq[:, WINDOW:WINDOW + BLK],
                         jnp.where(valid[:, WINDOW + BLK:], sq[:, WINDOW + BLK:], NEG)], axis=1)
                    m = jnp.maximum(jnp.max(sq, axis=-1, keepdims=True), sk)
                    e = jnp.exp2(sq - m)
                    dpair.append(jnp.sum(e, axis=-1, keepdims=True) + jnp.exp2(sk - m))
                    pcol.append(e.astype(BF16))
                prow.append(jnp.concatenate(pcol, axis=1))
                dens.append(jnp.where(left, dpair[0], dpair[1]))
            pm = jnp.concatenate(prow, axis=0)
            o2 = jnp.dot(pm, vbd, preferred_element_type=F32)
            for p in range(2):
                outs.append(o2[p * BLK:(p + 1) * BLK] / dens[p])
        o = jnp.concatenate(outs, axis=1)
        o_ref[rows, :] = _rms(o, g_ref[...]).astype(BF16)


def _attn_a(sink, qa, kva, g, batch, seq):
    t = qa.shape[0]
    nblk = seq // BLK
    nstep = nblk // A_SUB
    cur = lambda b, i: (b * nstep + i, 0)
    prev = lambda b, i: (b * nblk + jnp.maximum(i * A_SUB - 1, 0), 0)
    nxt = lambda b, i: (b * nblk + jnp.minimum((i + 1) * A_SUB, nblk - 1), 0)
    const = lambda b, i: (0, 0)
    kw = kva.shape[1]
    return pl.pallas_call(
        _attn_a_kernel,
        grid=(batch, nstep),
        in_specs=[pl.BlockSpec(memory_space=pltpu.SMEM),
                  pl.BlockSpec((A_SUB * BLK, A_Q_W), cur),
                  pl.BlockSpec((BLK, kw), prev),
                  pl.BlockSpec((A_SUB * BLK, kw), cur),
                  pl.BlockSpec((BLK, kw), nxt),
                  pl.BlockSpec((1, A_Q_W), const)],
        out_specs=pl.BlockSpec((A_SUB * BLK, A_Q_W), cur),
        out_shape=jax.ShapeDtypeStruct((t, A_Q_W), BF16),
        compiler_params=pltpu.CompilerParams(
            dimension_semantics=("arbitrary", "arbitrary"), vmem_limit_bytes=VMEM_LIMIT),
        name="attn_a",
    )(sink, qa, kva, kva, kva, g)


def _attn_b_kernel(ql_ref, qr_ref, kp_ref, kc_ref, kn_ref, vp_ref, vc_ref, vn_ref,
                   bias_ref, g_ref, o_ref):
    js = pl.program_id(1)
    ngrp = pl.num_programs(1) * B_SUB
    left = lax.broadcasted_iota(jnp.int32, (NB_TOK, LANES), 1) < HEAD_DIM

    def win_block(refs, idx, cols):
        p_ref, c_ref, n_ref = refs
        if idx == 0:
            return p_ref[:, cols]
        if idx <= B_SUB:
            return c_ref[(idx - 1) * NB_TOK:idx * NB_TOK, cols]
        return n_ref[:, cols]

    for j in range(B_SUB):
        grp = js * B_SUB + j
        cfg = jnp.where(grp == 0, 0, jnp.where(grp == ngrp - 1, 2, 1))
        rows = slice(j * NB_TOK, (j + 1) * NB_TOK)
        outs = []
        for p in range(B_HEADS // 2):
            sl = slice(p * LANES, (p + 1) * LANES)
            q2 = jnp.concatenate([ql_ref[rows, sl], qr_ref[rows, sl]], axis=0)
            kw = jnp.concatenate([win_block((kp_ref, kc_ref, kn_ref), j + w, sl) for w in range(3)], axis=0)
            vw = jnp.concatenate([win_block((vp_ref, vc_ref, vn_ref), j + w, sl) for w in range(3)], axis=0)
            s = lax.dot_general(q2, kw, (((1,), (1,)), ((), ())), preferred_element_type=F32)
            ps = []
            ls = []
            for hh in range(2):
                sq = s[hh * NB_TOK:(hh + 1) * NB_TOK] + bias_ref[cfg, 2 * p + hh]
                m = jnp.max(sq, axis=-1, keepdims=True)
                e = jnp.exp2(sq - m)
                ls.append(jnp.sum(e, axis=-1, keepdims=True))
                ps.append(e.astype(BF16))
            o2 = jnp.dot(jnp.concatenate(ps, axis=0), vw, preferred_element_type=F32)
            outs.append(jnp.where(left, o2[:NB_TOK] / ls[0], o2[NB_TOK:] / ls[1]))
        o = jnp.concatenate(outs, axis=1)
        o_ref[rows, :] = _rms(o, g_ref[...]).astype(BF16)


def _attn_b(qbl, qbr, kb, vb, bias, g, batch, seq):
    t = qbl.shape[0]
    ngrp = seq // NB_TOK
    nstep = ngrp // B_SUB
    cur = lambda b, j: (b * nstep + j, 0)
    prev = lambda b, j: (b * ngrp + jnp.maximum(j * B_SUB - 1, 0), 0)
    nxt = lambda b, j: (b * ngrp + jnp.minimum((j + 1) * B_SUB, ngrp - 1), 0)
    const = lambda b, j: (0, 0)
    blk = lambda im: pl.BlockSpec((NB_TOK, B_W), im)
    big = pl.BlockSpec((B_SUB * NB_TOK, B_W), cur)
    return pl.pallas_call(
        _attn_b_kernel,
        grid=(batch, nstep),
        in_specs=[big, big, blk(prev), big, blk(nxt), blk(prev), big, blk(nxt),
                  pl.BlockSpec(bias.shape, lambda b, j: (0, 0, 0, 0), pipeline_mode=pl.Buffered(1)),
                  pl.BlockSpec((1, B_W), const)],
        out_specs=big,
        out_shape=jax.ShapeDtypeStruct((t, B_W), BF16),
        compiler_params=pltpu.CompilerParams(
            dimension_semantics=("arbitrary", "arbitrary"), vmem_limit_bytes=VMEM_LIMIT),
        name="attn_b",
    )(qbl, qbr, kb, kb, kb, vb, vb, vb, bias, g)


def _nb_row_ok(cfg, qr, wr):
    if cfg == 0:
        return wr >= NB_ROWS
    if cfg == 2:
        return wr < NA_KH
    return 0 <= wr - qr < NA_KH


def _nb_bias_kernel(u_ref, o_ref):
    lane = lax.broadcasted_iota(jnp.int32, (GRID_W, LANES), 1)
    cq = lax.broadcasted_iota(jnp.int32, (GRID_W, LANES), 0)
    kc = lane & (GRID_W - 1)
    cs = jnp.clip(cq - NA_KW // 2, 0, GRID_W - NA_KW)
    col_ok = (kc >= cs) & (kc < cs + NA_KW)
    lo_half = lane < GRID_W
    neg = jnp.full((GRID_W, LANES), NEG, F32)
    off = NA_KH - 1 - NB_ROWS
    for qr in range(NB_ROWS):
        for k in range(NB_WIN // LANES):
            a = 2 * k - qr + off
            w = pltpu.roll(jnp.broadcast_to(u_ref[0, a:a + 1, :], (GRID_W, LANES)), 0, 1,
                           stride=1, stride_axis=0)
            for cfg in range(3):
                ok_lo = _nb_row_ok(cfg, qr, 2 * k)
                ok_hi = _nb_row_ok(cfg, qr, 2 * k + 1)
                if ok_lo and ok_hi:
                    blk = jnp.where(col_ok, w, neg)
                elif ok_lo:
                    blk = jnp.where(col_ok & lo_half, w, neg)
                elif ok_hi:
                    blk = jnp.where(col_ok & jnp.logical_not(lo_half), w, neg)
                else:
                    blk = neg
                o_ref[cfg, 0, qr * GRID_W:(qr + 1) * GRID_W, k * LANES:(k + 1) * LANES] = blk


def _nb_bias(rpb_l):
    r = rpb_l.astype(F32) * LOG2E
    heads, na, nb = r.shape
    half = nb // 2
    gap = (LANES - 2 * nb) // 2
    z = jnp.zeros((heads, na - 1, gap), F32)
    u = jnp.concatenate([r[:, :-1, half:], z, r[:, 1:, :], z, r[:, :-1, :half]], axis=-1)
    return pl.pallas_call(
        _nb_bias_kernel,
        grid=(heads,),
        in_specs=[pl.BlockSpec((1, na - 1, LANES), lambda h: (h, 0, 0))],
        out_specs=pl.BlockSpec((3, 1, NB_TOK, NB_WIN), lambda h: (0, h, 0, 0)),
        out_shape=jax.ShapeDtypeStruct((3, heads, NB_TOK, NB_WIN), F32),
        compiler_params=pltpu.CompilerParams(
            dimension_semantics=("arbitrary",), vmem_limit_bytes=VMEM_LIMIT),
        name="nb_bias",
    )(u)


def _ffn_kernel(nseq, final, ap_ref, a_ref, an_ref, bp_ref, b_ref, bn_ref, xp_ref, x_ref, xx_ref,
                wo_ref, gf_ref, wg_ref, wu_ref, cw_ref, cb_ref, wd_ref, gl_ref,
                o_ref, xn_ref, h_ref, act_ref):
    i = pl.program_id(0)
    tm = x_ref.shape[0]
    hal = FFN_HALO
    rows = tm + 2 * hal
    mix = jnp.concatenate(
        [jnp.concatenate([ap_ref[...], a_ref[...], an_ref[...]], axis=0),
         jnp.concatenate([bp_ref[...], b_ref[...], bn_ref[...]], axis=0)], axis=1)
    xc = jnp.concatenate([xp_ref[...], x_ref[...], xx_ref[...]], axis=0)
    h = xc + jnp.dot(mix, wo_ref[...], preferred_element_type=F32)
    h_ref[...] = h[hal:hal + tm]
    first = (i % nseq) == 0
    last = (i % nseq) == nseq - 1
    r = lax.broadcasted_iota(jnp.int32, (rows, 1), 0)
    outside = (first & (r < hal)) | (last & (r >= hal + tm))
    xn_ref[...] = jnp.where(outside, 0.0, _rms(h, gf_ref[...])).astype(BF16)

    for lo, hi in FFN_CHUNKS:
        g = jnp.dot(xn_ref[...], wg_ref[:, lo:hi], preferred_element_type=F32)
        gc = (pltpu.roll(g, 1, 0)[hal:hal + tm] * cw_ref[0:1, lo:hi]
              + g[hal:hal + tm] * cw_ref[1:2, lo:hi]
              + pltpu.roll(g, rows - 1, 0)[hal:hal + tm] * cw_ref[2:3, lo:hi]
              + cb_ref[:, lo:hi])
        u = jnp.dot(xn_ref[hal:hal + tm, :], wu_ref[:, lo:hi], preferred_element_type=F32)
        act_ref[:, lo:hi] = (gc / (1.0 + jnp.exp(-gc)) * u).astype(BF16)

    out = h_ref[...] + jnp.dot(act_ref[...], wd_ref[...], preferred_element_type=F32)
    o_ref[...] = _rms(out, gl_ref[...]) if final else out


def _ffn(ma, mb, x2, wo, gf, wg, wu, cw, cb, wd, gl, seq, final):
    t = x2.shape[0]
    tm = FFN_TM
    nseq = seq // tm
    r = tm // FFN_HALO
    nhal = t // FFN_HALO
    row = lambda i: (i, 0)
    prev = lambda i: (jnp.maximum(i * r - 1, 0), 0)
    nxt = lambda i: (jnp.minimum((i + 1) * r, nhal - 1), 0)
    const = lambda i: (0, 0)
    once = lambda shape: pl.BlockSpec(shape, const, pipeline_mode=pl.Buffered(1))
    halo3 = lambda w: [pl.BlockSpec((FFN_HALO, w), prev), pl.BlockSpec((tm, w), row),
                       pl.BlockSpec((FFN_HALO, w), nxt)]
    return pl.pallas_call(
        functools.partial(_ffn_kernel, nseq, final),
        grid=(t // tm,),
        in_specs=halo3(A_Q_W) + halo3(B_W) + halo3(D_MODEL) + [
            once(wo.shape), once((1, D_MODEL)), once(wg.shape), once(wu.shape), once(cw.shape),
            once(cb.shape), once(wd.shape), once((1, D_MODEL))],
        out_specs=pl.BlockSpec((tm, D_MODEL), row),
        out_shape=jax.ShapeDtypeStruct((t, D_MODEL), F32),
        scratch_shapes=[pltpu.VMEM((tm + 2 * FFN_HALO, D_MODEL), BF16),
                        pltpu.VMEM((tm, D_MODEL), F32),
                        pltpu.VMEM((tm, D_FF), BF16)],
        compiler_params=pltpu.CompilerParams(
            dimension_semantics=("arbitrary",), vmem_limit_bytes=VMEM_LIMIT),
        name="ffn",
    )(ma, ma, ma, mb, mb, mb, x2, x2, x2, wo, gf, wg, wu, cw, cb, wd, gl)


def _rope_tables(seq):
    f32 = np.float32
    pos = np.arange(seq, dtype=np.float64)
    inv_freq = ROPE_THETA ** (-np.arange(0, ROT_DIM, 2, dtype=np.float64) / ROT_DIM)
    ang = pos[:, None] * inv_freq[None, :]
    cos = np.cos(ang).astype(f32)
    sin = np.sin(ang).astype(f32)
    half = ROT_DIM // 2
    ones = np.ones((seq, HEAD_DIM - ROT_DIM), f32)
    zeros = np.zeros((seq, HEAD_DIM - ROT_DIM), f32)
    zh = np.zeros((seq, half), f32)
    c = np.concatenate([cos, cos, ones], axis=1)
    s1 = np.concatenate([zh, sin, zeros], axis=1)
    s2 = np.concatenate([-sin, zh, zeros], axis=1)
    rep = LANES // HEAD_DIM
    return tuple(jnp.asarray(np.tile(t, (1, rep))) for t in (c, s1, s2))


def kernel(x, norm_mix, w_in, sink, rpb, norm_a, norm_b, w_out, norm_ffn,
           w_gate, w_up, conv_w, conv_b, w_down, norm_final):
    batch, seq, d = x.shape
    depth = w_in.shape[0]
    assert d == D_MODEL and seq % PROJ_TM == 0 and seq % FFN_TM == 0 and seq % (A_SUB * BLK) == 0
    assert seq % (B_SUB * NB_TOK) == 0 and seq // GRID_W >= 3 * NB_ROWS
    c, s1, s2 = _rope_tables(seq)
    h = x.reshape(batch * seq, d)
    for l in range(depth):
        qa, kva, qbl, qbr, kb, vb, wo, wg, wu, wd = _proj(
            h, norm_mix[l][None], w_in[l].astype(BF16), c, s1, s2, seq,
            [m.astype(F32) for m in (w_out[l], w_gate[l], w_up[l], w_down[l])])
        ma = _attn_a(sink[l].astype(F32) * LOG2E, qa, kva, norm_a[l][None].astype(F32), batch, seq)
        mb = _attn_b(qbl, qbr, kb, vb, _nb_bias(rpb[l]), norm_b[l][None].astype(F32), batch, seq)
        h = _ffn(ma, mb, h, wo, norm_ffn[l][None], wg, wu, conv_w[l], conv_b[l][None], wd,
                 norm_final[None], seq, final=(l == depth - 1))
    return h.reshape(batch, seq, d)
```

```python
import functools
import math

import numpy as np
import jax
import jax.numpy as jnp
from jax import lax
from jax.experimental import pallas as pl
from jax.experimental.pallas import tpu as pltpu

D_MODEL = 1024
HEAD_DIM = 64
A_HEADS = 8
A_KV_HEADS = 2
B_HEADS = 8
A_Q_W = A_HEADS * HEAD_DIM
A_KV_W = A_KV_HEADS * HEAD_DIM
B_W = B_HEADS * HEAD_DIM
WINDOW = 128
BLK = 128
ROPE_THETA = 500000.0
ROT_DIM = HEAD_DIM // 4
GRID_W = 64
NA_KH = 8
NA_KW = 16
D_FF = 2816
EPS = 1e-6
NEG = -1e30
LOG2E = math.log2(math.e)

LANES = 128
BF16_ROWS = 16
VMEM_LIMIT = 56 * 1024 * 1024

F32 = jnp.float32
BF16 = jnp.bfloat16

PROJ_TM = 1024
A_SUB = 8
B_SUB = 4
FFN_TM = 1024
FFN_TF = 1024
NB_ROWS = 4
NB_TOK = NB_ROWS * GRID_W
NB_WIN = 3 * NB_TOK
FFN_HALO = 16
FFN_CHUNKS = tuple((lo, min(lo + FFN_TF, D_FF)) for lo in range(0, D_FF, FFN_TF))


def _rms(x, g):
    return x * lax.rsqrt(jnp.mean(x * x, axis=-1, keepdims=True) + EPS) * g


def _proj_kernel(x_ref, g_ref, w_ref, c_ref, s1_ref, s2_ref, *refs):
    wbf_ref = refs[-1]
    ncast = (len(refs) - 7) // 2
    cast_in = refs[:ncast]
    qa_ref, kva_ref, qbl_ref, qbr_ref, kb_ref, vb_ref = refs[ncast:ncast + 6]
    cast_out = refs[ncast + 6:-1]

    @pl.when(pl.program_id(0) == 0)
    def _():
        wbf_ref[...] = w_ref[...].astype(BF16)
    hn = _rms(x_ref[...], g_ref[...]).astype(BF16)
    c = c_ref[...]
    s1 = s1_ref[...]
    s2 = s2_ref[...]
    tm = hn.shape[0]
    left = lax.broadcasted_iota(jnp.int32, (tm, LANES), 1) < HEAD_DIM
    scale = LOG2E / math.sqrt(HEAD_DIM)

    def rope(t):
        return t * c + pltpu.roll(t, 8, 1) * s1 + pltpu.roll(t, LANES - 8, 1) * s2

    def mm(lo, hi):
        return jnp.dot(hn, wbf_ref[:, lo:hi], preferred_element_type=F32)

    qa = mm(0, A_Q_W)
    for j in range(A_Q_W // LANES):
        sl = slice(j * LANES, (j + 1) * LANES)
        qa_ref[:, sl] = (rope(qa[:, sl]) * scale).astype(BF16)

    kv = mm(A_Q_W, A_Q_W + 2 * A_KV_W)
    ka = rope(kv[:, :LANES])
    va = kv[:, LANES:]
    zero = jnp.zeros_like(ka)
    for base, t in ((0, ka), (4, va)):
        sw = pltpu.roll(t, HEAD_DIM, 1)
        parts = (jnp.where(left, t, zero), jnp.where(left, zero, sw),
                 jnp.where(left, sw, zero), jnp.where(left, zero, t))
        for n, p in enumerate(parts):
            kva_ref[:, (base + n) * LANES:(base + n + 1) * LANES] = p.astype(BF16)

    o = A_Q_W + 2 * A_KV_W
    qb = mm(o, o + B_W) * scale
    for j in range(B_W // LANES):
        sl = slice(j * LANES, (j + 1) * LANES)
        t = qb[:, sl]
        qbl_ref[:, sl] = jnp.where(left, t, zero).astype(BF16)
        qbr_ref[:, sl] = jnp.where(left, zero, t).astype(BF16)
    kb_ref[...] = mm(o + B_W, o + 2 * B_W).astype(BF16)
    vb_ref[...] = mm(o + 2 * B_W, o + 3 * B_W).astype(BF16)
    for src, dst in zip(cast_in, cast_out):
        dst[...] = src[...].astype(BF16)


def _slab_spec(shape, nsteps):
    rows, cols = shape
    nslab = max(n for n in range(1, nsteps + 1)
                if nsteps % n == 0 and rows % n == 0 and (rows // n) % BF16_ROWS == 0)
    rep = nsteps // nslab
    return pl.BlockSpec((rows // nslab, cols), lambda i: (i // rep, 0))


def _proj(x2, g, w, c, s1, s2, seq, to_cast):
    t = x2.shape[0]
    tm = PROJ_TM
    nseq = seq // tm
    nsteps = t // tm
    row = lambda i: (i, 0)
    const = lambda i: (0, 0)
    tab = lambda i: (i % nseq, 0)
    outs = [(A_Q_W, row), (8 * LANES, row), (B_W, row), (B_W, row), (B_W, row), (B_W, row)]
    slabs = [_slab_spec(m.shape, nsteps) for m in to_cast]
    return pl.pallas_call(
        _proj_kernel,
        grid=(nsteps,),
        in_specs=[pl.BlockSpec((tm, D_MODEL), row),
                  pl.BlockSpec((1, D_MODEL), const),
                  pl.BlockSpec(w.shape, const, pipeline_mode=pl.Buffered(1)),
                  pl.BlockSpec((tm, LANES), tab),
                  pl.BlockSpec((tm, LANES), tab),
                  pl.BlockSpec((tm, LANES), tab)] + slabs,
        out_specs=[pl.BlockSpec((tm, wd), im) for wd, im in outs] + slabs,
        out_shape=[jax.ShapeDtypeStruct((t, wd), BF16) for wd, _ in outs]
        + [jax.ShapeDtypeStruct(m.shape, BF16) for m in to_cast],
        scratch_shapes=[pltpu.VMEM(w.shape, BF16)],
        compiler_params=pltpu.CompilerParams(
            dimension_semantics=("arbitrary",), vmem_limit_bytes=VMEM_LIMIT),
        name="proj",
    )(x2, g, w, c, s1, s2, *to_cast)


def _attn_a_kernel(sink_ref, q_ref, kp_ref, kc_ref, kn_ref, g_ref, o_ref):
    i = pl.program_id(1)
    nstep = pl.num_programs(1)
    span = BLK + 2 * WINDOW
    a = lax.broadcasted_iota(jnp.int32, (BLK, span), 0)
    b = lax.broadcasted_iota(jnp.int32, (BLK, span), 1)
    d = b - a
    band = (d >= 0) & (d <= 2 * WINDOW)
    left = lax.broadcasted_iota(jnp.int32, (BLK, LANES), 1) < HEAD_DIM

    def key_block(idx, cols):
        if idx == 0:
            return kp_ref[:, cols]
        if idx <= A_SUB:
            return kc_ref[(idx - 1) * BLK:idx * BLK, cols]
        return kn_ref[:, cols]

    for j in range(A_SUB):
        valid = band
        if j == 0:
            valid = valid & (b >= jnp.where(i == 0, WINDOW, 0))
        if j == A_SUB - 1:
            valid = valid & (b < jnp.where(i == nstep - 1, WINDOW + BLK, span))
        rows = slice(j * BLK, (j + 1) * BLK)
        outs = []
        for g in range(A_KV_HEADS):
            def stack(col):
                return jnp.concatenate(
                    [key_block(j + w, slice((col + n) * LANES, (col + n + 1) * LANES))
                     for n in range(2) for w in range(3)], axis=0)
            kbd = stack(2 * g)
            vbd = stack(4 + 2 * g)
            q2 = jnp.concatenate(
                [q_ref[rows, (2 * g + p) * LANES:(2 * g + p + 1) * LANES] for p in range(2)], axis=0)
            s = lax.dot_general(q2, kbd, (((1,), (1,)), ((), ())), preferred_element_type=F32)
            prow = []
            dens = []
            for p in range(2):
                pcol = []
                dpair = []
                for hh in range(2):
                    sk = sink_ref[4 * g + 2 * p + hh]
                    sq = s[p * BLK:(p + 1) * BLK, hh * span:(hh + 1) * span]
                    sq = jnp.concatenate(
                        [jnp.where(valid[:, :WINDOW], sq[:, :WINDOW], NEG), sq[:, WINDOW:WINDOW + BLK],
                         jnp.where(valid[:, WINDOW + BLK:], sq[:, WINDOW + BLK:], NEG)], axis=1)
                    m = jnp.maximum(jnp.max(sq, axis=-1, keepdims=True), sk)
                    e = jnp.exp2(sq - m)
                    dpair.append(jnp.sum(e, axis=-1, keepdims=True) + jnp.exp2(sk - m))
                    pcol.append(e.astype(BF16))
                prow.append(jnp.concatenate(pcol, axis=1))
                dens.append(jnp.where(left, dpair[0], dpair[1]))
            pm = jnp.concatenate(prow, axis=0)
            o2 = jnp.dot(pm, vbd, preferred_element_type=F32)
            for p in range(2):
                outs.append(o2[p * BLK:(p + 1) * BLK] / dens[p])
        o = jnp.concatenate(outs, axis=1)
        o_ref[rows, :] = _rms(o, g_ref[...]).astype(BF16)


def _attn_a(sink, qa, kva, g, batch, seq):
    t = qa.shape[0]
    nblk = seq // BLK
    nstep = nblk // A_SUB
    cur = lambda b, i: (b * nstep + i, 0)
    prev = lambda b, i: (b * nblk + jnp.maximum(i * A_SUB - 1, 0), 0)
    nxt = lambda b, i: (b * nblk + jnp.minimum((i + 1) * A_SUB, nblk - 1), 0)
    const = lambda b, i: (0, 0)
    kw = kva.shape[1]
    return pl.pallas_call(
        _attn_a_kernel,
        grid=(batch, nstep),
        in_specs=[pl.BlockSpec(memory_space=pltpu.SMEM),
                  pl.BlockSpec((A_SUB * BLK, A_Q_W), cur),
                  pl.BlockSpec((BLK, kw), prev),
                  pl.BlockSpec((A_SUB * BLK, kw), cur),
                  pl.BlockSpec((BLK, kw), nxt),
                  pl.BlockSpec((1, A_Q_W), const)],
        out_specs=pl.BlockSpec((A_SUB * BLK, A_Q_W), cur),
        out_shape=jax.ShapeDtypeStruct((t, A_Q_W), BF16),
        compiler_params=pltpu.CompilerParams(
            dimension_semantics=("arbitrary", "arbitrary"), vmem_limit_bytes=VMEM_LIMIT),
        name="attn_a",
    )(sink, qa, kva, kva, kva, g)


def _attn_b_kernel(ql_ref, qr_ref, kp_ref, kc_ref, kn_ref, vp_ref, vc_ref, vn_ref,
                   bias_ref, g_ref, o_ref):
    js = pl.program_id(1)
    ngrp = pl.num_programs(1) * B_SUB
    left = lax.broadcasted_iota(jnp.int32, (NB_TOK, LANES), 1) < HEAD_DIM

    def win_block(refs, idx, cols):
        p_ref, c_ref, n_ref = refs
        if idx == 0:
            return p_ref[:, cols]
        if idx <= B_SUB:
            return c_ref[(idx - 1) * NB_TOK:idx * NB_TOK, cols]
        return n_ref[:, cols]

    for j in range(B_SUB):
        grp = js * B_SUB + j
        cfg = jnp.where(grp == 0, 0, jnp.where(grp == ngrp - 1, 2, 1))
        rows = slice(j * NB_TOK, (j + 1) * NB_TOK)
        outs = []
        for p in range(B_HEADS // 2):
            sl = slice(p * LANES, (p + 1) * LANES)
            q2 = jnp.concatenate([ql_ref[rows, sl], qr_ref[rows, sl]], axis=0)
            kw = jnp.concatenate([win_block((kp_ref, kc_ref, kn_ref), j + w, sl) for w in range(3)], axis=0)
            vw = jnp.concatenate([win_block((vp_ref, vc_ref, vn_ref), j + w, sl) for w in range(3)], axis=0)
            s = lax.dot_general(q2, kw, (((1,), (1,)), ((), ())), preferred_element_type=F32)
            ps = []
            ls = []
            for hh in range(2):
                sq = s[hh * NB_TOK:(hh + 1) * NB_TOK] + bias_ref[cfg, 2 * p + hh]
                m = jnp.max(sq, axis=-1, keepdims=True)
                e = jnp.exp2(sq - m)
                ls.append(jnp.sum(e, axis=-1, keepdims=True))
                ps.append(e.astype(BF16))
            o2 = jnp.dot(jnp.concatenate(ps, axis=0), vw, preferred_element_type=F32)
            outs.append(jnp.where(left, o2[:NB_TOK] / ls[0], o2[NB_TOK:] / ls[1]))
        o = jnp.concatenate(outs, axis=1)
        o_ref[rows, :] = _rms(o, g_ref[...]).astype(BF16)


def _attn_b(qbl, qbr, kb, vb, bias, g, batch, seq):
    t = qbl.shape[0]
    ngrp = seq // NB_TOK
    nstep = ngrp // B_SUB
    cur = lambda b, j: (b * nstep + j, 0)
    prev = lambda b, j: (b * ngrp + jnp.maximum(j * B_SUB - 1, 0), 0)
    nxt = lambda b, j: (b * ngrp + jnp.minimum((j + 1) * B_SUB, ngrp - 1), 0)
    const = lambda b, j: (0, 0)
    blk = lambda im: pl.BlockSpec((NB_TOK, B_W), im)
    big = pl.BlockSpec((B_SUB * NB_TOK, B_W), cur)
    return pl.pallas_call(
        _attn_b_kernel,
        grid=(batch, nstep),
        in_specs=[big, big, blk(prev), big, blk(nxt), blk(prev), big, blk(nxt),
                  pl.BlockSpec(bias.shape, lambda b, j: (0, 0, 0, 0), pipeline_mode=pl.Buffered(1)),
                  pl.BlockSpec((1, B_W), const)],
        out_specs=big,
        out_shape=jax.ShapeDtypeStruct((t, B_W), BF16),
        compiler_params=pltpu.CompilerParams(
            dimension_semantics=("arbitrary", "arbitrary"), vmem_limit_bytes=VMEM_LIMIT),
        name="attn_b",
    )(qbl, qbr, kb, kb, kb, vb, vb, vb, bias, g)


def _nb_row_ok(cfg, qr, wr):
    if cfg == 0:
        return wr >= NB_ROWS
    if cfg == 2:
        return wr < NA_KH
    return 0 <= wr - qr < NA_KH


def _nb_bias_kernel(u_ref, o_ref):
    lane = lax.broadcasted_iota(jnp.int32, (GRID_W, LANES), 1)
    cq = lax.broadcasted_iota(jnp.int32, (GRID_W, LANES), 0)
    kc = lane & (GRID_W - 1)
    cs = jnp.clip(cq - NA_KW // 2, 0, GRID_W - NA_KW)
    col_ok = (kc >= cs) & (kc < cs + NA_KW)
    lo_half = lane < GRID_W
    neg = jnp.full((GRID_W, LANES), NEG, F32)
    off = NA_KH - 1 - NB_ROWS
    for qr in range(NB_ROWS):
        for k in range(NB_WIN // LANES):
            a = 2 * k - qr + off
            w = pltpu.roll(jnp.broadcast_to(u_ref[0, a:a + 1, :], (GRID_W, LANES)), 0, 1,
                           stride=1, stride_axis=0)
            for cfg in range(3):
                ok_lo = _nb_row_ok(cfg, qr, 2 * k)
                ok_hi = _nb_row_ok(cfg, qr, 2 * k + 1)
                if ok_lo and ok_hi:
                    blk = jnp.where(col_ok, w, neg)
                elif ok_lo:
                    blk = jnp.where(col_ok & lo_half, w, neg)
                elif ok_hi:
                    blk = jnp.where(col_ok & jnp.logical_not(lo_half), w, neg)
                else:
                    blk = neg
                o_ref[cfg, 0, qr * GRID_W:(qr + 1) * GRID_W, k * LANES:(k + 1) * LANES] = blk


def _nb_bias(rpb_l):
    r = rpb_l.astype(F32) * LOG2E
    heads, na, nb = r.shape
    half = nb // 2
    gap = (LANES - 2 * nb) // 2
    z = jnp.zeros((heads, na - 1, gap), F32)
    u = jnp.concatenate([r[:, :-1, half:], z, r[:, 1:, :], z, r[:, :-1, :half]], axis=-1)
    return pl.pallas_call(
        _nb_bias_kernel,
        grid=(heads,),
        in_specs=[pl.BlockSpec((1, na - 1, LANES), lambda h: (h, 0, 0))],
        out_specs=pl.BlockSpec((3, 1, NB_TOK, NB_WIN), lambda h: (0, h, 0, 0)),
        out_shape=jax.ShapeDtypeStruct((3, heads, NB_TOK, NB_WIN), F32),
        compiler_params=pltpu.CompilerParams(
            dimension_semantics=("arbitrary",), vmem_limit_bytes=VMEM_LIMIT),
        name="nb_bias",
    )(u)


def _ffn_kernel(nseq, final, ap_ref, a_ref, an_ref, bp_ref, b_ref, bn_ref, xp_ref, x_ref, xx_ref,
                wo_ref, gf_ref, wg_ref, wu_ref, cw_ref, cb_ref, wd_ref, gl_ref,
                o_ref, xn_ref, h_ref, act_ref):
    i = pl.program_id(0)
    tm = x_ref.shape[0]
    hal = FFN_HALO
    rows = tm + 2 * hal
    mix = jnp.concatenate(
        [jnp.concatenate([ap_ref[...], a_ref[...], an_ref[...]], axis=0),
         jnp.concatenate([bp_ref[...], b_ref[...], bn_ref[...]], axis=0)], axis=1)
    xc = jnp.concatenate([xp_ref[...], x_ref[...], xx_ref[...]], axis=0)
    h = xc + jnp.dot(mix, wo_ref[...], preferred_element_type=F32)
    h_ref[...] = h[hal:hal + tm]
    first = (i % nseq) == 0
    last = (i % nseq) == nseq - 1
    r = lax.broadcasted_iota(jnp.int32, (rows, 1), 0)
    outside = (first & (r < hal)) | (last & (r >= hal + tm))
    xn_ref[...] = jnp.where(outside, 0.0, _rms(h, gf_ref[...])).astype(BF16)

    for lo, hi in FFN_CHUNKS:
        g = jnp.dot(xn_ref[...], wg_ref[:, lo:hi], preferred_element_type=F32)
        gc = (pltpu.roll(g, 1, 0)[hal:hal + tm] * cw_ref[0:1, lo:hi]
              + g[hal:hal + tm] * cw_ref[1:2, lo:hi]
              + pltpu.roll(g, rows - 1, 0)[hal:hal + tm] * cw_ref[2:3, lo:hi]
              + cb_ref[:, lo:hi])
        u = jnp.dot(xn_ref[hal:hal + tm, :], wu_ref[:, lo:hi], preferred_element_type=F32)
        act_ref[:, lo:hi] = (gc / (1.0 + jnp.exp(-gc)) * u).astype(BF16)

    out = h_ref[...] + jnp.dot(act_ref[...], wd_ref[...], preferred_element_type=F32)
    o_ref[...] = _rms(out, gl_ref[...]) if final else out


def _ffn(ma, mb, x2, wo, gf, wg, wu, cw, cb, wd, gl, seq, final):
    t = x2.shape[0]
    tm = FFN_TM
    nseq = seq // tm
    r = tm // FFN_HALO
    nhal = t // FFN_HALO
    row = lambda i: (i, 0)
    prev = lambda i: (jnp.maximum(i * r - 1, 0), 0)
    nxt = lambda i: (jnp.minimum((i + 1) * r, nhal - 1), 0)
    const = lambda i: (0, 0)
    once = lambda shape: pl.BlockSpec(shape, const, pipeline_mode=pl.Buffered(1))
    halo3 = lambda w: [pl.BlockSpec((FFN_HALO, w), prev), pl.BlockSpec((tm, w), row),
                       pl.BlockSpec((FFN_HALO, w), nxt)]
    return pl.pallas_call(
        functools.partial(_ffn_kernel, nseq, final),
        grid=(t // tm,),
        in_specs=halo3(A_Q_W) + halo3(B_W) + halo3(D_MODEL) + [
            once(wo.shape), once((1, D_MODEL)), once(wg.shape), once(wu.shape), once(cw.shape),
            once(cb.shape), once(wd.shape), once((1, D_MODEL))],
        out_specs=pl.BlockSpec((tm, D_MODEL), row),
        out_shape=jax.ShapeDtypeStruct((t, D_MODEL), F32),
        scratch_shapes=[pltpu.VMEM((tm + 2 * FFN_HALO, D_MODEL), BF16),
                        pltpu.VMEM((tm, D_MODEL), F32),
                        pltpu.VMEM((tm, D_FF), BF16)],
        compiler_params=pltpu.CompilerParams(
            dimension_semantics=("arbitrary",), vmem_limit_bytes=VMEM_LIMIT),
        name="ffn",
    )(ma, ma, ma, mb, mb, mb, x2, x2, x2, wo, gf, wg, wu, cw, cb, wd, gl)


def _rope_tables(seq):
    f32 = np.float32
    pos = np.arange(seq, dtype=np.float64)
    inv_freq = ROPE_THETA ** (-np.arange(0, ROT_DIM, 2, dtype=np.float64) / ROT_DIM)
    ang = pos[:, None] * inv_freq[None, :]
    cos = np.cos(ang).astype(f32)
    sin = np.sin(ang).astype(f32)
    half = ROT_DIM // 2
    ones = np.ones((seq, HEAD_DIM - ROT_DIM), f32)
    zeros = np.zeros((seq, HEAD_DIM - ROT_DIM), f32)
    zh = np.zeros((seq, half), f32)
    c = np.concatenate([cos, cos, ones], axis=1)
    s1 = np.concatenate([zh, sin, zeros], axis=1)
    s2 = np.concatenate([-sin, zh, zeros], axis=1)
    rep = LANES // HEAD_DIM
    return tuple(jnp.asarray(np.tile(t, (1, rep))) for t in (c, s1, s2))


def kernel(x, norm_mix, w_in, sink, rpb, norm_a, norm_b, w_out, norm_ffn,
           w_gate, w_up, conv_w, conv_b, w_down, norm_final):
    batch, seq, d = x.shape
    depth = w_in.shape[0]
    assert d == D_MODEL and seq % PROJ_TM == 0 and seq % FFN_TM == 0 and seq % (A_SUB * BLK) == 0
    assert seq % (B_SUB * NB_TOK) == 0 and seq // GRID_W >= 3 * NB_ROWS
    c, s1, s2 = _rope_tables(seq)
    h = x.reshape(batch * seq, d)
    for l in range(depth):
        qa, kva, qbl, qbr, kb, vb, wo, wg, wu, wd = _proj(
            h, norm_mix[l][None], w_in[l].astype(F32), c, s1, s2, seq,
            [m.astype(F32) for m in (w_out[l], w_gate[l], w_up[l], w_down[l])])
        ma = _attn_a(sink[l].astype(F32) * LOG2E, qa, kva, norm_a[l][None].astype(F32), batch, seq)
        mb = _attn_b(qbl, qbr, kb, vb, _nb_bias(rpb[l]), norm_b[l][None].astype(F32), batch, seq)
        h = _ffn(ma, mb, h, wo, norm_ffn[l][None], wg, wu, conv_w[l], conv_b[l][None], wd,
                 norm_final[None], seq, final=(l == depth - 1))
    return h.reshape(batch, seq, d)
```
